```python
import math
import jax, jax.numpy as jnp
from jax import lax
import numpy as np

D_MODEL = 4096
BATCH = 1
SEQ = 8192
DEPTH = 1

ATTN_HEADS = 8
ATTN_HEAD_DIM = 128
ATTN_V_DIM = 2 * ATTN_HEAD_DIM
ATTN_WIDTH = ATTN_HEADS * ATTN_V_DIM
Q_COLS = ATTN_HEADS * 2 * ATTN_HEAD_DIM
K_COLS = ATTN_HEADS * 2 * ATTN_HEAD_DIM
Q_BLOCK = 128
ROPE_THETA = 500000.0
ROPE_DIM = ATTN_HEAD_DIM // 4

SGU_WIDTH = D_MODEL // 2
SGU_GROUPS = 8
SGU_GROUP_DIM = SGU_WIDTH // SGU_GROUPS
CHUNK = 128

D_FF = 4 * D_MODEL

ALPHA = (2.0 * DEPTH) ** 0.25
BETA = (8.0 * DEPTH) ** -0.25
LN_EPS = 1e-5

IN_WIDTHS = (Q_COLS, K_COLS, ATTN_WIDTH, SGU_WIDTH, SGU_WIDTH, D_MODEL, D_MODEL)
IN_COLS = sum(IN_WIDTHS)
SPLIT_POINTS = tuple(int(c) for c in np.cumsum(IN_WIDTHS)[:-1])

kernel_name = "hybrid_diffattn_chunked_sgu_gated_deepnorm"


def layer_norm(x, g, b):
    xf = x.astype(jnp.float32)
    mu = jnp.mean(xf, axis=-1, keepdims=True)
    var = jnp.mean(jnp.square(xf - mu), axis=-1, keepdims=True)
    y = (xf - mu) * lax.rsqrt(var + LN_EPS) * g.astype(jnp.float32) + b.astype(jnp.float32)
    return y.astype(x.dtype)


def partial_rope(t, pos):
    half = ROPE_DIM // 2
    inv_freq = ROPE_THETA ** (-jnp.arange(0, ROPE_DIM, 2, dtype=jnp.float32) / ROPE_DIM)
    ang = pos.astype(jnp.float32)[:, None] * inv_freq[None, :]
    cos = jnp.cos(ang)[None, :, None, None, :]
    sin = jnp.sin(ang)[None, :, None, None, :]
    rot = t[..., :ROPE_DIM].astype(jnp.float32)
    r1, r2 = rot[..., :half], rot[..., half:]
    rotated = jnp.concatenate([r1 * cos - r2 * sin, r2 * cos + r1 * sin], axis=-1)
    return jnp.concatenate([rotated.astype(t.dtype), t[..., ROPE_DIM:]], axis=-1)


def diff_attention(q, k, v, lam, subln_w, lambda_init):
    B, S = q.shape[0], q.shape[1]
    qh = q.transpose(0, 2, 3, 1, 4)
    kh = k.transpose(0, 2, 3, 1, 4)
    vh = v.transpose(0, 2, 1, 3)
    scale = ATTN_HEAD_DIM ** -0.5
    k_pos = jnp.arange(S)

    def one_block(start):
        qb = lax.dynamic_slice_in_dim(qh, start, Q_BLOCK, axis=3)
        s = jnp.einsum('bhcqd,bhckd->bhcqk', qb, kh).astype(jnp.float32) * scale
        q_pos = start + jnp.arange(Q_BLOCK)
        causal = q_pos[:, None] >= k_pos[None, :]
        s = jnp.where(causal, s, -jnp.inf)
        p = jax.nn.softmax(s, axis=-1)
        a = p[:, :, 0] - lam * p[:, :, 1]
        return jnp.einsum('bhqk,bhkd->bhqd', a.astype(vh.dtype), vh)

    starts = jnp.arange(S // Q_BLOCK) * Q_BLOCK
    o = lax.map(one_block, starts)
    o = o.transpose(1, 0, 3, 2, 4).reshape(B, S, ATTN_HEADS, ATTN_V_DIM)
    of = o.astype(jnp.float32)
    of = of * lax.rsqrt(jnp.mean(jnp.square(of), axis=-1, keepdims=True) + LN_EPS)
    of = of * subln_w.astype(jnp.float32) * (1.0 - lambda_init)
    return of.reshape(B, S, ATTN_WIDTH).astype(v.dtype)


def chunked_sgu(u, s, ln_g, ln_b, w_s, b_s):
    B, S, _ = s.shape
    s = layer_norm(s, ln_g, ln_b)
    sc = s.reshape(B, S // CHUNK, CHUNK, SGU_GROUPS, SGU_GROUP_DIM)
    causal = jnp.tril(jnp.ones((CHUNK, CHUNK), dtype=bool))
    w = jnp.where(causal[None], w_s, 0.0)
    mixed = jnp.einsum('gts,bcsgd->bctgd', w, sc) + b_s.T[None, None, :, :, None]
    return u * mixed.reshape(B, S, SGU_WIDTH)


def setup_inputs(seed: int = 0) -> dict:
    key = jax.random.key(seed)
    ks = jax.random.split(key, 24)
    f32 = jnp.float32
    nrm = lambda k, shape: jax.random.normal(k, shape, dtype=f32)
    L = DEPTH
    return {
        "x": nrm(ks[0], (BATCH, SEQ, D_MODEL)),
        "w_in": nrm(ks[1], (L, D_MODEL, IN_COLS)) * D_MODEL ** -0.5,
        "lambda_q1": nrm(ks[2], (L, ATTN_HEAD_DIM)) * 0.1,
        "lambda_k1": nrm(ks[3], (L, ATTN_HEAD_DIM)) * 0.1,
        "lambda_q2": nrm(ks[4], (L, ATTN_HEAD_DIM)) * 0.1,
        "lambda_k2": nrm(ks[5], (L, ATTN_HEAD_DIM)) * 0.1,
        "subln_w": 1.0 + 0.02 * nrm(ks[6], (L, ATTN_V_DIM)),
        "sgu_ln_g": 1.0 + 0.02 * nrm(ks[7], (L, SGU_WIDTH)),
        "sgu_ln_b": 0.02 * nrm(ks[8], (L, SGU_WIDTH)),
        "w_spatial": nrm(ks[9], (L, SGU_GROUPS, CHUNK, CHUNK)) * CHUNK ** -0.5,
        "b_spatial": 1.0 + 0.02 * nrm(ks[10], (L, SGU_GROUPS, CHUNK)),
        "w_proj_attn": nrm(ks[11], (L, ATTN_WIDTH, D_MODEL)) * ATTN_WIDTH ** -0.5,
        "w_proj_sgu": nrm(ks[12], (L, SGU_WIDTH, D_MODEL)) * SGU_WIDTH ** -0.5,
        "w_out": nrm(ks[13], (L, D_MODEL, D_MODEL)) * (D_MODEL ** -0.5 * BETA),
        "ln1_g": 1.0 + 0.02 * nrm(ks[14], (L, D_MODEL)),
        "ln1_b": 0.02 * nrm(ks[15], (L, D_MODEL)),
        "w_mlp_in": nrm(ks[16], (L, D_MODEL, D_FF)) * D_MODEL ** -0.5,
        "b_mlp_in": 0.02 * nrm(ks[17], (L, D_FF)),
        "w_mlp_out": nrm(ks[18], (L, D_FF, D_MODEL)) * (D_FF ** -0.5 * BETA),
        "b_mlp_out": 0.02 * nrm(ks[19], (L, D_MODEL)),
        "ln2_g": 1.0 + 0.02 * nrm(ks[20], (L, D_MODEL)),
        "ln2_b": 0.02 * nrm(ks[21], (L, D_MODEL)),
    }


def reference(x, w_in, lambda_q1, lambda_k1, lambda_q2, lambda_k2, subln_w,
              sgu_ln_g, sgu_ln_b, w_spatial, b_spatial, w_proj_attn, w_proj_sgu,
              w_out, ln1_g, ln1_b, w_mlp_in, b_mlp_in, w_mlp_out, b_mlp_out,
              ln2_g, ln2_b):
    B, S, _ = x.shape
    pos = jnp.arange(S)
    h = x
    for l in range(DEPTH):
        lambda_init = 0.8 - 0.6 * math.exp(-0.3 * l)
        proj = h @ w_in[l]
        q, k, v, u, s, g_a, g_b = jnp.split(proj, SPLIT_POINTS, axis=-1)
        q = partial_rope(q.reshape(B, S, ATTN_HEADS, 2, ATTN_HEAD_DIM), pos)
        k = partial_rope(k.reshape(B, S, ATTN_HEADS, 2, ATTN_HEAD_DIM), pos)
        v = v.reshape(B, S, ATTN_HEADS, ATTN_V_DIM)
        lam = (jnp.exp(jnp.sum(lambda_q1[l].astype(jnp.float32) * lambda_k1[l].astype(jnp.float32)))
               - jnp.exp(jnp.sum(lambda_q2[l].astype(jnp.float32) * lambda_k2[l].astype(jnp.float32)))
               + lambda_init)
        y_a = diff_attention(q, k, v, lam, subln_w[l], lambda_init)
        y_b = chunked_sgu(jax.nn.gelu(u), jax.nn.gelu(s), sgu_ln_g[l], sgu_ln_b[l],
                          w_spatial[l], b_spatial[l])
        merged = (jax.nn.sigmoid(g_a) * (y_a @ w_proj_attn[l])
                  + jax.nn.sigmoid(g_b) * (y_b @ w_proj_sgu[l]))
        mix = merged @ w_out[l]
        h = layer_norm(ALPHA * h + mix, ln1_g[l], ln1_b[l])
        z = jax.nn.relu(h @ w_mlp_in[l] + b_mlp_in[l])
        ff = (z * z) @ w_mlp_out[l] + b_mlp_out[l]
        h = layer_norm(ALPHA * h + ff, ln2_g[l], ln2_b[l])
    return h
```

```python
import functools
import math

import numpy as np
import jax
import jax.numpy as jnp
from jax import lax
from jax.experimental import pallas as pl
from jax.experimental.pallas import tpu as pltpu

_F32 = jnp.float32
_BF16 = jnp.bfloat16

_LANES = 128
_VMEM_LIMIT_BYTES = 60 * 1024 * 1024

_HEADS = 8
_HEAD_DIM = 128
_V_DIM = 2 * _HEAD_DIM
_ROPE_DIM = _HEAD_DIM // 4
_ROPE_HALF = _ROPE_DIM // 2
_ROPE_THETA = 500000.0
_SGU_GROUPS = 8
_CHUNK = 128
_DEPTH = 1
_ALPHA = (2.0 * _DEPTH) ** 0.25
_LN_EPS = 1e-5
_SEG = 2048


def _params(sem):
    return pltpu.CompilerParams(dimension_semantics=sem, vmem_limit_bytes=_VMEM_LIMIT_BYTES)


def _dot(a, b):
    return jnp.dot(a, b, preferred_element_type=_F32)


def _layer_norm_rows(x, g, b):
    mu = jnp.mean(x, axis=-1, keepdims=True)
    xc = x - mu
    var = jnp.mean(xc * xc, axis=-1, keepdims=True)
    return xc * lax.rsqrt(var + _LN_EPS) * g + b


def _cast_kernel(x_ref, o_ref):
    o_ref[...] = x_ref[...].astype(o_ref.dtype)


def _cast_bf16(x, tm=512):
    m, n = x.shape
    return pl.pallas_call(
        _cast_kernel,
        grid=(m // tm,),
        in_specs=[pl.BlockSpec((tm, n), lambda i: (i, 0))],
        out_specs=pl.BlockSpec((tm, n), lambda i: (i, 0)),
        out_shape=jax.ShapeDtypeStruct((m, n), _BF16),
        compiler_params=_params(("arbitrary",)),
        name="cast_x",
    )(x)


def _rope_lanes(t, cos, sin_up, sin_dn):
    return (t * cos
            + pltpu.roll(t, _ROPE_HALF, 1) * sin_up
            + pltpu.roll(t, _LANES - _ROPE_HALF, 1) * sin_dn)


def _proj_kernel(x_ref, w_ref, cos_ref, sup_ref, sdn_ref, o_ref, *, tn, q_scale):
    j = pl.program_id(1)
    per_seg = _SEG // tn
    seg = j // per_seg
    acc = _dot(x_ref[...], w_ref[...].astype(_BF16))

    def rope_store(scale):
        cos, sup, sdn = cos_ref[...], sup_ref[...], sdn_ref[...]
        for c in range(tn // _LANES):
            sl = slice(c * _LANES, (c + 1) * _LANES)
            r = _rope_lanes(acc[:, sl], cos, sup, sdn)
            if scale is not None:
                r = r * scale
            o_ref[:, sl] = r.astype(o_ref.dtype)

    @pl.when(seg == 0)
    def _():
        rope_store(q_scale)

    @pl.when(seg == 1)
    def _():
        rope_store(None)

    @pl.when(seg == 2)
    def _():
        o_ref[...] = acc.astype(o_ref.dtype)

    @pl.when((seg == 3) | (seg == 4))
    def _():
        o_ref[...] = jax.nn.gelu(acc).astype(o_ref.dtype)

    @pl.when(seg >= 5)
    def _():
        o_ref[...] = jax.nn.sigmoid(acc).astype(o_ref.dtype)


def _proj(xb, w, cos, sup, sdn, tm=1024, tn=512):
    m, k = xb.shape
    n = w.shape[1]
    kern = functools.partial(_proj_kernel, tn=tn, q_scale=_HEAD_DIM ** -0.5)
    return pl.pallas_call(
        kern,
        grid=(m // tm, n // tn),
        in_specs=[
            pl.BlockSpec((tm, k), lambda i, j: (i, 0)),
            pl.BlockSpec((k, tn), lambda i, j: (0, j)),
            pl.BlockSpec((tm, _LANES), lambda i, j: (i, 0)),
            pl.BlockSpec((tm, _LANES), lambda i, j: (i, 0)),
            pl.BlockSpec((tm, _LANES), lambda i, j: (i, 0)),
        ],
        out_specs=pl.BlockSpec((tm, tn), lambda i, j: (i, j)),
        out_shape=jax.ShapeDtypeStruct((m, n), _BF16),
        compiler_params=_params(("arbitrary", "arbitrary")),
        name="proj",
    )(xb, w, cos, sup, sdn)


def _attn_kernel(qi_tab, ki_tab, lq1_ref, lk1_ref, lq2_ref, lk2_ref, subw_ref,
                 q_ref, k_ref, v_ref, o_ref, m_ref, l_ref, acc_ref, *, tq, tk, lambda_init):
    p_idx = pl.program_id(1)
    qi = qi_tab[p_idx]
    ki = ki_tab[p_idx]
    last_k = ((qi + 1) * tq - 1) // tk

    @pl.when(ki == 0)
    def _():
        m_ref[...] = jnp.full(m_ref.shape, -jnp.inf, _F32)
        l_ref[...] = jnp.zeros(l_ref.shape, _F32)
        acc_ref[...] = jnp.zeros(acc_ref.shape, _F32)

    def step(masked):
        v = v_ref[...]
        for c in range(2):
            sl = slice(c * _HEAD_DIM, (c + 1) * _HEAD_DIM)
            s = lax.dot_general(q_ref[:, sl], k_ref[:, sl], (((1,), (1,)), ((), ())),
                                preferred_element_type=_F32)
            if masked:
                row = qi * tq + lax.broadcasted_iota(jnp.int32, (tq, tk), 0)
                col = ki * tk + lax.broadcasted_iota(jnp.int32, (tq, tk), 1)
                s = jnp.where(row >= col, s, -jnp.inf)
            m_prev = m_ref[c]
            m_new = jnp.maximum(m_prev, jnp.max(s, axis=1, keepdims=True))
            alpha = jnp.exp(m_prev - m_new)
            p = jnp.exp(s - m_new)
            l_ref[c] = alpha * l_ref[c] + jnp.sum(p, axis=1, keepdims=True)
            acc_ref[c] = alpha * acc_ref[c] + _dot(p.astype(_BF16), v)
            m_ref[c] = m_new

    needs_mask = (ki + 1) * tk - 1 > qi * tq

    @pl.when(needs_mask)
    def _():
        step(True)

    @pl.when(jnp.logical_not(needs_mask))
    def _():
        step(False)

    @pl.when(ki == last_k)
    def _():
        lam = (jnp.exp(jnp.sum(lq1_ref[...] * lk1_ref[...], axis=1, keepdims=True))
               - jnp.exp(jnp.sum(lq2_ref[...] * lk2_ref[...], axis=1, keepdims=True))
               + lambda_init)
        o = acc_ref[0] / l_ref[0] - lam * (acc_ref[1] / l_ref[1])
        o = o * lax.rsqrt(jnp.mean(o * o, axis=-1, keepdims=True) + _LN_EPS)
        o = o * subw_ref[...] * (1.0 - lambda_init)
        o_ref[...] = o.astype(o_ref.dtype)


def _causal_pairs(n_q, tq, tk):
    qs, ks = [], []
    for qi in range(n_q):
        for ki in range(((qi + 1) * tq - 1) // tk + 1):
            qs.append(qi)
            ks.append(ki)
    return np.asarray(qs, np.int32), np.asarray(ks, np.int32)


def _attention(proj, lq1, lk1, lq2, lk2, subw, lambda_init, tq=1024, tk=1024):
    s_len = proj.shape[0]
    qs, ks = _causal_pairs(s_len // tq, tq, tk)
    k_blk0 = _SEG // _V_DIM
    v_blk0 = 2 * _SEG // _V_DIM
    vec = lambda n: pl.BlockSpec((1, n), lambda h, p, qt, kt: (0, 0))
    kern = functools.partial(_attn_kernel, tq=tq, tk=tk, lambda_init=lambda_init)
    return pl.pallas_call(
        kern,
        grid_spec=pltpu.PrefetchScalarGridSpec(
            num_scalar_prefetch=2,
            grid=(_HEADS, len(qs)),
            in_specs=[
                vec(_HEAD_DIM), vec(_HEAD_DIM), vec(_HEAD_DIM), vec(_HEAD_DIM), vec(_V_DIM),
                pl.BlockSpec((tq, _V_DIM), lambda h, p, qt, kt: (qt[p], h)),
                pl.BlockSpec((tk, _V_DIM), lambda h, p, qt, kt: (kt[p], k_blk0 + h)),
                pl.BlockSpec((tk, _V_DIM), lambda h, p, qt, kt: (kt[p], v_blk0 + h)),
            ],
            out_specs=pl.BlockSpec((tq, _V_DIM), lambda h, p, qt, kt: (qt[p], h)),
            scratch_shapes=[
                pltpu.VMEM((2, tq, 1), _F32),
                pltpu.VMEM((2, tq, 1), _F32),
                pltpu.VMEM((2, tq, _V_DIM), _F32),
            ],
        ),
        out_shape=jax.ShapeDtypeStruct((s_len, _HEADS * _V_DIM), _BF16),
        compiler_params=_params(("arbitrary", "arbitrary")),
        name="diff_attn",
    )(jnp.asarray(qs), jnp.asarray(ks), lq1, lk1, lq2, lk2, subw, proj, proj, proj)


def _sgu_kernel(u_ref, s_ref, g_ref, b_ref, w_ref, bs_ref, o_ref, *, tm):
    gd = _SEG // _SGU_GROUPS
    sn = _layer_norm_rows(s_ref[...].astype(_F32), g_ref[...], b_ref[...]).astype(_BF16)
    row = lax.broadcasted_iota(jnp.int32, (_CHUNK, _CHUNK), 0)
    col = lax.broadcasted_iota(jnp.int32, (_CHUNK, _CHUNK), 1)
    for g in range(_SGU_GROUPS):
        w = jnp.where(row >= col, w_ref[g], 0.0).astype(_BF16)
        bias = bs_ref[:, g:g + 1]
        cs = slice(g * gd, (g + 1) * gd)
        for c in range(tm // _CHUNK):
            rs = slice(c * _CHUNK, (c + 1) * _CHUNK)
            mixed = _dot(w, sn[rs, cs]) + bias
            o_ref[rs, cs] = (u_ref[rs, cs].astype(_F32) * mixed).astype(o_ref.dtype)


def _sgu(proj, ln_g, ln_b, w_s, b_s_t, tm=512):
    s_len = proj.shape[0]
    u_blk = 3
    return pl.pallas_call(
        functools.partial(_sgu_kernel, tm=tm),
        grid=(s_len // tm,),
        in_specs=[
            pl.BlockSpec((tm, _SEG), lambda i: (i, u_blk)),
            pl.BlockSpec((tm, _SEG), lambda i: (i, u_blk + 1)),
            pl.BlockSpec((1, _SEG), lambda i: (0, 0)),
            pl.BlockSpec((1, _SEG), lambda i: (0, 0)),
            pl.BlockSpec((_SGU_GROUPS, _CHUNK, _CHUNK), lambda i: (0, 0, 0)),
            pl.BlockSpec((_CHUNK, _SGU_GROUPS), lambda i: (0, 0)),
        ],
        out_specs=pl.BlockSpec((tm, _SEG), lambda i: (i, 0)),
        out_shape=jax.ShapeDtypeStruct((s_len, _SEG), _BF16),
        compiler_params=_params(("arbitrary",)),
        name="sgu",
    )(proj, proj, ln_g, ln_b, w_s, b_s_t)


def _merge_kernel(ya_ref, yb_ref, wa_ref, wb_ref, ga_ref, gb_ref, o_ref):
    a = _dot(ya_ref[...], wa_ref[...].astype(_BF16))
    b = _dot(yb_ref[...], wb_ref[...].astype(_BF16))
    o_ref[...] = (ga_ref[...].astype(_F32) * a + gb_ref[...].astype(_F32) * b).astype(o_ref.dtype)


def _merge(y_a, y_b, w_a, w_b, proj, tm=1024, tn=512):
    m, k = y_a.shape
    n = w_a.shape[1]
    ga_blk0 = 5 * _SEG // tn
    gb_blk0 = ga_blk0 + n // tn
    return pl.pallas_call(
        _merge_kernel,
        grid=(m // tm, n // tn),
        in_specs=[
            pl.BlockSpec((tm, k), lambda i, j: (i, 0)),
            pl.BlockSpec((tm, k), lambda i, j: (i, 0)),
            pl.BlockSpec((k, tn), lambda i, j: (0, j)),
            pl.BlockSpec((k, tn), lambda i, j: (0, j)),
            pl.BlockSpec((tm, tn), lambda i, j: (i, ga_blk0 + j)),
            pl.BlockSpec((tm, tn), lambda i, j: (i, gb_blk0 + j)),
        ],
        out_specs=pl.BlockSpec((tm, tn), lambda i, j: (i, j)),
        out_shape=jax.ShapeDtypeStruct((m, n), _BF16),
        compiler_params=_params(("arbitrary", "arbitrary")),
        name="merge",
    )(y_a, y_b, w_a, w_b, proj, proj)


def _outproj_kernel(a_ref, w_ref, x_ref, o_ref):
    o_ref[...] = _ALPHA * x_ref[...] + _dot(a_ref[...], w_ref[...].astype(_BF16))


def _outproj(merged, w, x, tm=1024, tn=512):
    m, k = merged.shape
    n = w.shape[1]
    return pl.pallas_call(
        _outproj_kernel,
        grid=(m // tm, n // tn),
        in_specs=[
            pl.BlockSpec((tm, k), lambda i, j: (i, 0)),
            pl.BlockSpec((k, tn), lambda i, j: (0, j)),
            pl.BlockSpec((tm, tn), lambda i, j: (i, j)),
        ],
        out_specs=pl.BlockSpec((tm, tn), lambda i, j: (i, j)),
        out_shape=jax.ShapeDtypeStruct((m, n), _F32),
        compiler_params=_params(("arbitrary", "arbitrary")),
        name="outproj",
    )(merged, w, x)


def _ln_dual_kernel(x_ref, g_ref, b_ref, o_ref, ob_ref):
    y = _layer_norm_rows(x_ref[...], g_ref[...], b_ref[...])
    o_ref[...] = y
    ob_ref[...] = y.astype(ob_ref.dtype)


def _ln_kernel(x_ref, g_ref, b_ref, o_ref):
    o_ref[...] = _layer_norm_rows(x_ref[...], g_ref[...], b_ref[...])


def _layer_norm(x, g, b, with_bf16, tm=256):
    m, n = x.shape
    row = pl.BlockSpec((tm, n), lambda i: (i, 0))
    vec = pl.BlockSpec((1, n), lambda i: (0, 0))
    if with_bf16:
        kern, out_specs = _ln_dual_kernel, [row, row]
        out_shape = [jax.ShapeDtypeStruct((m, n), _F32), jax.ShapeDtypeStruct((m, n), _BF16)]
    else:
        kern, out_specs, out_shape = _ln_kernel, row, jax.ShapeDtypeStruct((m, n), _F32)
    return pl.pallas_call(
        kern,
        grid=(m // tm,),
        in_specs=[row, vec, vec],
        out_specs=out_specs,
        out_shape=out_shape,
        compiler_params=_params(("arbitrary",)),
        name="layer_norm",
    )(x, g, b)


def _mlp_in_kernel(a_ref, w_ref, b_ref, o_ref):
    z = jnp.maximum(_dot(a_ref[...], w_ref[...].astype(_BF16)) + b_ref[...], 0.0)
    o_ref[...] = (z * z).astype(o_ref.dtype)


def _mlp_in(hb, w, b, tm=1024, tn=512):
    m, k = hb.shape
    n = w.shape[1]
    return pl.pallas_call(
        _mlp_in_kernel,
        grid=(m // tm, n // tn),
        in_specs=[
            pl.BlockSpec((tm, k), lambda i, j: (i, 0)),
            pl.BlockSpec((k, tn), lambda i, j: (0, j)),
            pl.BlockSpec((1, tn), lambda i, j: (0, j)),
        ],
        out_specs=pl.BlockSpec((tm, tn), lambda i, j: (i, j)),
        out_shape=jax.ShapeDtypeStruct((m, n), _BF16),
        compiler_params=_params(("arbitrary", "arbitrary")),
        name="mlp_in",
    )(hb, w, b)


def _mlp_out_kernel(z_ref, w_ref, h_ref, b_ref, o_ref, acc_ref, *, nk):
    kk = pl.program_id(2)
    part = _dot(z_ref[...], w_ref[...].astype(_BF16))

    @pl.when(kk == 0)
    def _():
        acc_ref[...] = part

    @pl.when(kk > 0)
    def _():
        acc_ref[...] += part

    @pl.when(kk == nk - 1)
    def _():
        o_ref[...] = _ALPHA * h_ref[...] + acc_ref[...] + b_ref[...]


def _mlp_out(z, w, h, b, tm=1024, tn=1024, tk=2048):
    m, k = z.shape
    n = w.shape[1]
    nk = k // tk
    return pl.pallas_call(
        functools.partial(_mlp_out_kernel, nk=nk),
        grid=(m // tm, n // tn, nk),
        in_specs=[
            pl.BlockSpec((tm, tk), lambda i, j, kk: (i, kk)),
            pl.BlockSpec((tk, tn), lambda i, j, kk: (kk, j)),
            pl.BlockSpec((tm, tn), lambda i, j, kk: (i, j)),
            pl.BlockSpec((1, tn), lambda i, j, kk: (0, j)),
        ],
        out_specs=pl.BlockSpec((tm, tn), lambda i, j, kk: (i, j)),
        out_shape=jax.ShapeDtypeStruct((m, n), _F32),
        scratch_shapes=[pltpu.VMEM((tm, tn), _F32)],
        compiler_params=_params(("arbitrary", "arbitrary", "arbitrary")),
        name="mlp_out",
    )(z, w, h, b)


def _rope_tables(s_len):
    inv_freq = _ROPE_THETA ** (-jnp.arange(0, _ROPE_DIM, 2, dtype=_F32) / _ROPE_DIM)
    ang = jnp.arange(s_len).astype(_F32)[:, None] * inv_freq[None, :]
    cos, sin = jnp.cos(ang), jnp.sin(ang)
    ones = jnp.ones((s_len, _LANES - _ROPE_DIM), _F32)
    zeros_h = jnp.zeros((s_len, _ROPE_HALF), _F32)
    zeros_r = jnp.zeros((s_len, _LANES - _ROPE_DIM), _F32)
    cos_t = jnp.concatenate([cos, cos, ones], axis=1)
    sin_up = jnp.concatenate([zeros_h, sin, zeros_r], axis=1)
    sin_dn = jnp.concatenate([-sin, zeros_h, zeros_r], axis=1)
    return cos_t, sin_up, sin_dn


def kernel(x, w_in, lambda_q1, lambda_k1, lambda_q2, lambda_k2, subln_w, sgu_ln_g, sgu_ln_b,
           w_spatial, b_spatial, w_proj_attn, w_proj_sgu, w_out, ln1_g, ln1_b, w_mlp_in, b_mlp_in,
           w_mlp_out, b_mlp_out, ln2_g, ln2_b):
    bsz, s_len, d = x.shape
    assert bsz == 1 and w_in.shape[0] == _DEPTH == 1
    h = x.reshape(s_len, d)
    l = 0
    lambda_init = 0.8 - 0.6 * math.exp(-0.3 * l)
    cos_t, sin_up, sin_dn = _rope_tables(s_len)

    hb = _cast_bf16(h)
    proj = _proj(hb, w_in[l], cos_t, sin_up, sin_dn)
    y_a = _attention(proj, lambda_q1[l][None], lambda_k1[l][None], lambda_q2[l][None],
                     lambda_k2[l][None], subln_w[l][None], lambda_init)
    y_b = _sgu(proj, sgu_ln_g[l][None], sgu_ln_b[l][None], w_spatial[l], b_spatial[l].T)
    merged = _merge(y_a, y_b, w_proj_attn[l], w_proj_sgu[l], proj)
    pre1 = _outproj(merged, w_out[l], h)
    h1, h1b = _layer_norm(pre1, ln1_g[l][None], ln1_b[l][None], with_bf16=True)
    z = _mlp_in(h1b, w_mlp_in[l], b_mlp_in[l][None])
    pre2 = _mlp_out(z, w_mlp_out[l], h1, b_mlp_out[l][None])
    out = _layer_norm(pre2, ln2_g[l][None], ln2_b[l][None], with_bf16=False)
    return out.reshape(bsz, s_len, d)
```

```python
import functools
import math

import numpy as np
import jax
import jax.numpy as jnp
from jax import lax
from jax.experimental import pallas as pl
from jax.experimental.pallas import tpu as pltpu

_F32 = jnp.float32
_BF16 = jnp.bfloat16

_LANES = 128
_VMEM_LIMIT_BYTES = 60 * 1024 * 1024

_HEADS = 8
_HEAD_DIM = 128
_V_DIM = 2 * _HEAD_DIM
_ROPE_DIM = _HEAD_DIM // 4
_ROPE_HALF = _ROPE_DIM // 2
_ROPE_THETA = 500000.0
_SGU_GROUPS = 8
_CHUNK = 128
_DEPTH = 1
_ALPHA = (2.0 * _DEPTH) ** 0.25
_LN_EPS = 1e-5
_SEG = 2048

_TM = 2048
_TN = 512


def _params(sem):
    return pltpu.CompilerParams(dimension_semantics=sem, vmem_limit_bytes=_VMEM_LIMIT_BYTES)


def _dot(a, b):
    return jnp.dot(a, b, preferred_element_type=_F32)


def _layer_norm_rows(x, g, b):
    mu = jnp.mean(x, axis=-1, keepdims=True)
    xc = x - mu
    var = jnp.mean(xc * xc, axis=-1, keepdims=True)
    return xc * lax.rsqrt(var + _LN_EPS) * g + b


def _cast_kernel(x_ref, o_ref):
    o_ref[...] = x_ref[...].astype(o_ref.dtype)


def _cast_bf16(x, tm=512):
    m, n = x.shape
    return pl.pallas_call(
        _cast_kernel,
        grid=(m // tm,),
        in_specs=[pl.BlockSpec((tm, n), lambda i: (i, 0))],
        out_specs=pl.BlockSpec((tm, n), lambda i: (i, 0)),
        out_shape=jax.ShapeDtypeStruct((m, n), _BF16),
        compiler_params=_params(("arbitrary",)),
        name="cast_x",
    )(x)


def _linear_kernel(a_ref, w_ref, *rest, epilogue):
    *extras, o_ref = rest
    acc = _dot(a_ref[...], w_ref[...].astype(_BF16))
    epilogue(acc, o_ref, *extras)


def _row_spec(tm, width):
    return pl.BlockSpec((tm, width), lambda i, j: (i, 0))


def _col_spec(tn, blk0=0):
    return pl.BlockSpec((1, tn), lambda i, j: (0, blk0 + j))


def _tile_spec(tm, tn, blk0=0):
    return pl.BlockSpec((tm, tn), lambda i, j: (i, blk0 + j))


def _linear(name, a, w, col0, n_cols, epilogue, extras=(), out_dtype=_BF16, tm=_TM, tn=_TN):
    m, k = a.shape
    blk0 = col0 // tn
    arrays = [x for x, _ in extras]
    specs = [s for _, s in extras]
    return pl.pallas_call(
        functools.partial(_linear_kernel, epilogue=epilogue),
        grid=(m // tm, n_cols // tn),
        in_specs=[
            pl.BlockSpec((tm, k), lambda i, j: (i, 0), pipeline_mode=pl.Buffered(1)),
            pl.BlockSpec((k, tn), lambda i, j: (0, blk0 + j)),
            *specs,
        ],
        out_specs=pl.BlockSpec((tm, tn), lambda i, j: (i, j)),
        out_shape=jax.ShapeDtypeStruct((m, n_cols), out_dtype),
        compiler_params=_params(("arbitrary", "arbitrary")),
        name=name,
    )(a, w, *arrays)


def _store_epilogue(acc, o_ref):
    o_ref[...] = acc.astype(o_ref.dtype)


def _gelu_epilogue(acc, o_ref):
    o_ref[...] = jax.nn.gelu(acc).astype(o_ref.dtype)


def _sigmoid_epilogue(acc, o_ref):
    o_ref[...] = jax.nn.sigmoid(acc).astype(o_ref.dtype)


def _rope_epilogue(acc, o_ref, cos_ref, sup_ref, sdn_ref, scale_ref):
    cos, sup, sdn = cos_ref[...], sup_ref[...], sdn_ref[...]
    for c in range(acc.shape[1] // _LANES):
        sl = slice(c * _LANES, (c + 1) * _LANES)
        t = acc[:, sl]
        r = (t * cos
             + pltpu.roll(t, _ROPE_HALF, 1) * sup
             + pltpu.roll(t, _LANES - _ROPE_HALF, 1) * sdn)
        o_ref[:, sl] = (r * scale_ref[:, sl]).astype(o_ref.dtype)


def _residual_epilogue(acc, o_ref, x_ref):
    o_ref[...] = _ALPHA * x_ref[...] + acc


def _relu2_epilogue(acc, o_ref, b_ref):
    z = jnp.maximum(acc + b_ref[...], 0.0)
    o_ref[...] = (z * z).astype(o_ref.dtype)


def _attn_kernel(qi_tab, ki_tab, lq1_ref, lk1_ref, lq2_ref, lk2_ref, subw_ref,
                 q_ref, k_ref, v_ref, o_ref, m_ref, l_ref, acc_ref, *, tq, tk, rq, lambda_init):
    p_idx = pl.program_id(1)
    qi = qi_tab[p_idx]
    ki = ki_tab[p_idx]
    last_k = ((qi + 1) * tq - 1) // tk

    @pl.when(ki == 0)
    def _():
        m_ref[...] = jnp.full(m_ref.shape, -jnp.inf, _F32)
        l_ref[...] = jnp.zeros(l_ref.shape, _F32)
        acc_ref[...] = jnp.zeros(acc_ref.shape, _F32)

    def step(masked):
        v = v_ref[...]
        if masked:
            delta = (lax.broadcasted_iota(jnp.int32, (rq, tk), 0)
                     - lax.broadcasted_iota(jnp.int32, (rq, tk), 1))
        for c in range(2):
            sl = slice(c * _HEAD_DIM, (c + 1) * _HEAD_DIM)
            k = k_ref[:, sl]
            for r in range(tq // rq):
                rows = slice(r * rq, (r + 1) * rq)
                s = lax.dot_general(q_ref[rows, sl], k, (((1,), (1,)), ((), ())),
                                    preferred_element_type=_F32)
                if masked:
                    s = jnp.where(delta >= ki * tk - qi * tq - r * rq, s, -jnp.inf)
                m_prev = m_ref[c, rows]
                m_new = jnp.maximum(m_prev, jnp.max(s, axis=1, keepdims=True))
                alpha = jnp.exp2(m_prev - m_new)
                p = jnp.exp2(s - m_new)
                l_ref[c, rows] = alpha * l_ref[c, rows] + jnp.sum(p, axis=1, keepdims=True)
                acc_ref[c, rows] = alpha * acc_ref[c, rows] + _dot(p.astype(_BF16), v)
                m_ref[c, rows] = m_new

    needs_mask = (ki + 1) * tk - 1 > qi * tq

    @pl.when(needs_mask)
    def _():
        step(True)

    @pl.when(jnp.logical_not(needs_mask))
    def _():
        step(False)

    @pl.when(ki == last_k)
    def _():
        lam = (jnp.exp(jnp.sum(lq1_ref[...] * lk1_ref[...], axis=1, keepdims=True))
               - jnp.exp(jnp.sum(lq2_ref[...] * lk2_ref[...], axis=1, keepdims=True))
               + lambda_init)
        o = acc_ref[0] / l_ref[0] - lam * (acc_ref[1] / l_ref[1])
        o = o * lax.rsqrt(jnp.mean(o * o, axis=-1, keepdims=True) + _LN_EPS)
        o = o * subw_ref[...] * (1.0 - lambda_init)
        o_ref[...] = o.astype(o_ref.dtype)


def _causal_pairs(n_q, tq, tk):
    qs, ks = [], []
    for qi in range(n_q):
        for ki in range(((qi + 1) * tq - 1) // tk + 1):
            qs.append(qi)
            ks.append(ki)
    return np.asarray(qs, np.int32), np.asarray(ks, np.int32)


def _attention(qk, v, lq1, lk1, lq2, lk2, subw, lambda_init, tq=1024, tk=1024, rq=256):
    s_len = qk.shape[0]
    qs, ks = _causal_pairs(s_len // tq, tq, tk)
    k_blk0 = _SEG // _V_DIM
    vec = lambda n: pl.BlockSpec((1, n), lambda h, p, qt, kt: (0, 0))
    kern = functools.partial(_attn_kernel, tq=tq, tk=tk, rq=rq, lambda_init=lambda_init)
    return pl.pallas_call(
        kern,
        grid_spec=pltpu.PrefetchScalarGridSpec(
            num_scalar_prefetch=2,
            grid=(_HEADS, len(qs)),
            in_specs=[
                vec(_HEAD_DIM), vec(_HEAD_DIM), vec(_HEAD_DIM), vec(_HEAD_DIM), vec(_V_DIM),
                pl.BlockSpec((tq, _V_DIM), lambda h, p, qt, kt: (qt[p], h)),
                pl.BlockSpec((tk, _V_DIM), lambda h, p, qt, kt: (kt[p], k_blk0 + h)),
                pl.BlockSpec((tk, _V_DIM), lambda h, p, qt, kt: (kt[p], h)),
            ],
            out_specs=pl.BlockSpec((tq, _V_DIM), lambda h, p, qt, kt: (qt[p], h)),
            scratch_shapes=[
                pltpu.VMEM((2, tq, 1), _F32),
                pltpu.VMEM((2, tq, 1), _F32),
                pltpu.VMEM((2, tq, _V_DIM), _F32),
            ],
        ),
        out_shape=jax.ShapeDtypeStruct((s_len, _HEADS * _V_DIM), _BF16),
        compiler_params=_params(("arbitrary", "arbitrary")),
        name="diff_attn",
    )(jnp.asarray(qs), jnp.asarray(ks), lq1, lk1, lq2, lk2, subw, qk, qk, v)


def _sgu_kernel(u_ref, s_ref, g_ref, b_ref, w_ref, bs_ref, o_ref, *, tm):
    gd = _SEG // _SGU_GROUPS
    sn = _layer_norm_rows(s_ref[...].astype(_F32), g_ref[...], b_ref[...]).astype(_BF16)
    row = lax.broadcasted_iota(jnp.int32, (_CHUNK, _CHUNK), 0)
    col = lax.broadcasted_iota(jnp.int32, (_CHUNK, _CHUNK), 1)
    for g in range(_SGU_GROUPS):
        w = jnp.where(row >= col, w_ref[g], 0.0).astype(_BF16)
        bias = bs_ref[:, g:g + 1]
        cs = slice(g * gd, (g + 1) * gd)
        for c in range(tm // _CHUNK):
            rs = slice(c * _CHUNK, (c + 1) * _CHUNK)
            mixed = _dot(w, sn[rs, cs]) + bias
            o_ref[rs, cs] = (u_ref[rs, cs].astype(_F32) * mixed).astype(o_ref.dtype)


def _sgu(us, ln_g, ln_b, w_s, b_s_t, tm=512):
    s_len = us.shape[0]
    return pl.pallas_call(
        functools.partial(_sgu_kernel, tm=tm),
        grid=(s_len // tm,),
        in_specs=[
            pl.BlockSpec((tm, _SEG), lambda i: (i, 0)),
            pl.BlockSpec((tm, _SEG), lambda i: (i, 1)),
            pl.BlockSpec((1, _SEG), lambda i: (0, 0)),
            pl.BlockSpec((1, _SEG), lambda i: (0, 0)),
            pl.BlockSpec((_SGU_GROUPS, _CHUNK, _CHUNK), lambda i: (0, 0, 0)),
            pl.BlockSpec((_CHUNK, _SGU_GROUPS), lambda i: (0, 0)),
        ],
        out_specs=pl.BlockSpec((tm, _SEG), lambda i: (i, 0)),
        out_shape=jax.ShapeDtypeStruct((s_len, _SEG), _BF16),
        compiler_params=_params(("arbitrary",)),
        name="sgu",
    )(us, us, ln_g, ln_b, w_s, b_s_t)


def _merge_kernel(ya_ref, yb_ref, wa_ref, wb_ref, ga_ref, gb_ref, o_ref):
    a = _dot(ya_ref[...], wa_ref[...].astype(_BF16))
    b = _dot(yb_ref[...], wb_ref[...].astype(_BF16))
    o_ref[...] = (ga_ref[...].astype(_F32) * a + gb_ref[...].astype(_F32) * b).astype(o_ref.dtype)


def _merge(y_a, y_b, w_a, w_b, gates, tm=_TM, tn=_TN):
    m, k = y_a.shape
    n = w_a.shape[1]
    resident = lambda: pl.BlockSpec((tm, k), lambda i, j: (i, 0), pipeline_mode=pl.Buffered(1))
    return pl.pallas_call(
        _merge_kernel,
        grid=(m // tm, n // tn),
        in_specs=[
            resident(),
            resident(),
            pl.BlockSpec((k, tn), lambda i, j: (0, j)),
            pl.BlockSpec((k, tn), lambda i, j: (0, j)),
            _tile_spec(tm, tn),
            _tile_spec(tm, tn, n // tn),
        ],
        out_specs=pl.BlockSpec((tm, tn), lambda i, j: (i, j)),
        out_shape=jax.ShapeDtypeStruct((m, n), _BF16),
        compiler_params=_params(("arbitrary", "arbitrary")),
        name="merge",
    )(y_a, y_b, w_a, w_b, gates, gates)


def _ln_bf16_kernel(x_ref, g_ref, b_ref, o_ref):
    o_ref[...] = _layer_norm_rows(x_ref[...], g_ref[...], b_ref[...]).astype(o_ref.dtype)


def _ln_bf16(x, g, b, tm=256):
    m, n = x.shape
    row = pl.BlockSpec((tm, n), lambda i: (i, 0))
    vec = pl.BlockSpec((1, n), lambda i: (0, 0))
    return pl.pallas_call(
        _ln_bf16_kernel,
        grid=(m // tm,),
        in_specs=[row, vec, vec],
        out_specs=row,
        out_shape=jax.ShapeDtypeStruct((m, n), _BF16),
        compiler_params=_params(("arbitrary",)),
        name="ln1",
    )(x, g, b)


def _final_kernel(pre_ref, ff_ref, g1_ref, b1_ref, bo_ref, g2_ref, b2_ref, o_ref):
    h1 = _layer_norm_rows(pre_ref[...], g1_ref[...], b1_ref[...])
    o_ref[...] = _layer_norm_rows(_ALPHA * h1 + (ff_ref[...] + bo_ref[...]), g2_ref[...], b2_ref[...])


def _final(pre1, ff, g1, b1, b_out, g2, b2, tm=256):
    m, n = pre1.shape
    row = pl.BlockSpec((tm, n), lambda i: (i, 0))
    vec = pl.BlockSpec((1, n), lambda i: (0, 0))
    return pl.pallas_call(
        _final_kernel,
        grid=(m // tm,),
        in_specs=[row, row, vec, vec, vec, vec, vec],
        out_specs=row,
        out_shape=jax.ShapeDtypeStruct((m, n), _F32),
        compiler_params=_params(("arbitrary",)),
        name="final_ln",
    )(pre1, ff, g1, b1, b_out, g2, b2)


def _mlp_out_kernel(z_ref, w_ref, o_ref):
    kk = pl.program_id(2)

    @pl.when(kk == 0)
    def _():
        o_ref[...] = _dot(z_ref[...], w_ref[...].astype(_BF16))

    @pl.when(kk > 0)
    def _():
        o_ref[...] += _dot(z_ref[...], w_ref[...].astype(_BF16))


def _mlp_out(z, w, tm=_TM, tn=1024, tk=1024):
    m, k = z.shape
    n = w.shape[1]
    return pl.pallas_call(
        _mlp_out_kernel,
        grid=(m // tm, n // tn, k // tk),
        in_specs=[
            pl.BlockSpec((tm, tk), lambda i, j, kk: (i, kk)),
            pl.BlockSpec((tk, tn), lambda i, j, kk: (kk, j)),
        ],
        out_specs=pl.BlockSpec((tm, tn), lambda i, j, kk: (i, j)),
        out_shape=jax.ShapeDtypeStruct((m, n), _F32),
        compiler_params=_params(("arbitrary", "arbitrary", "arbitrary")),
        name="mlp_out",
    )(z, w)


def _rope_tables(s_len):
    inv_freq = _ROPE_THETA ** (-jnp.arange(0, _ROPE_DIM, 2, dtype=_F32) / _ROPE_DIM)
    ang = jnp.arange(s_len).astype(_F32)[:, None] * inv_freq[None, :]
    cos, sin = jnp.cos(ang), jnp.sin(ang)
    ones = jnp.ones((s_len, _LANES - _ROPE_DIM), _F32)
    zeros_h = jnp.zeros((s_len, _ROPE_HALF), _F32)
    zeros_r = jnp.zeros((s_len, _LANES - _ROPE_DIM), _F32)
    cos_t = jnp.concatenate([cos, cos, ones], axis=1)
    sin_up = jnp.concatenate([zeros_h, sin, zeros_r], axis=1)
    sin_dn = jnp.concatenate([-sin, zeros_h, zeros_r], axis=1)
    return cos_t, sin_up, sin_dn


def kernel(x, w_in, lambda_q1, lambda_k1, lambda_q2, lambda_k2, subln_w, sgu_ln_g, sgu_ln_b,
           w_spatial, b_spatial, w_proj_attn, w_proj_sgu, w_out, ln1_g, ln1_b, w_mlp_in, b_mlp_in,
           w_mlp_out, b_mlp_out, ln2_g, ln2_b):
    bsz, s_len, d = x.shape
    assert bsz == 1 and w_in.shape[0] == _DEPTH == 1
    h = x.reshape(s_len, d)
    l = 0
    lambda_init = 0.8 - 0.6 * math.exp(-0.3 * l)
    cos_t, sin_up, sin_dn = _rope_tables(s_len)
    q_scale = _HEAD_DIM ** -0.5 * math.log2(math.e)
    qk_scale = jnp.concatenate([jnp.full((1, _SEG), q_scale, _F32), jnp.ones((1, _SEG), _F32)], axis=1)

    hb = _cast_bf16(h)
    w_in_l = w_in[l]
    qk = _linear("proj_qk", hb, w_in_l, 0, 2 * _SEG, _rope_epilogue,
                 extras=[(cos_t, _row_spec(_TM, _LANES)), (sin_up, _row_spec(_TM, _LANES)),
                         (sin_dn, _row_spec(_TM, _LANES)), (qk_scale, _col_spec(_TN))])
    v = _linear("proj_v", hb, w_in_l, 2 * _SEG, _SEG, _store_epilogue)
    us = _linear("proj_us", hb, w_in_l, 3 * _SEG, 2 * _SEG, _gelu_epilogue)
    gates = _linear("proj_gates", hb, w_in_l, 5 * _SEG, 2 * d, _sigmoid_epilogue)

    y_a = _attention(qk, v, lambda_q1[l][None], lambda_k1[l][None], lambda_q2[l][None],
                     lambda_k2[l][None], subln_w[l][None], lambda_init)
    y_b = _sgu(us, sgu_ln_g[l][None], sgu_ln_b[l][None], w_spatial[l], b_spatial[l].T)
    merged = _merge(y_a, y_b, w_proj_attn[l], w_proj_sgu[l], gates)
    pre1 = _linear("outproj", merged, w_out[l], 0, d, _residual_epilogue,
                   extras=[(h, _tile_spec(_TM, _TN))], out_dtype=_F32)
    h1b = _ln_bf16(pre1, ln1_g[l][None], ln1_b[l][None])
    z = _linear("mlp_in", h1b, w_mlp_in[l], 0, w_mlp_in.shape[2], _relu2_epilogue,
                extras=[(b_mlp_in[l][None], _col_spec(_TN))])
    ff = _mlp_out(z, w_mlp_out[l])
    out = _final(pre1, ff, ln1_g[l][None], ln1_b[l][None], b_mlp_out[l][None],
                 ln2_g[l][None], ln2_b[l][None])
    return out.reshape(bsz, s_len, d)
```

```python
import functools
import math

import numpy as np
import jax
import jax.numpy as jnp
from jax import lax
from jax.experimental import pallas as pl
from jax.experimental.pallas import tpu as pltpu

_F32 = jnp.float32
_BF16 = jnp.bfloat16

_LANES = 128
_VMEM_LIMIT_BYTES = 60 * 1024 * 1024

_HEADS = 8
_HEAD_DIM = 128
_V_DIM = 2 * _HEAD_DIM
_ROPE_DIM = _HEAD_DIM // 4
_ROPE_HALF = _ROPE_DIM // 2
_ROPE_THETA = 500000.0
_SGU_GROUPS = 8
_CHUNK = 128
_DEPTH = 1
_ALPHA = (2.0 * _DEPTH) ** 0.25
_LN_EPS = 1e-5
_SEG = 2048

_TM = 1024
_TN = 512


def _params(sem):
    return pltpu.CompilerParams(dimension_semantics=sem, vmem_limit_bytes=_VMEM_LIMIT_BYTES)


def _dot(a, b):
    return jnp.dot(a, b, preferred_element_type=_F32)


def _layer_norm_rows(x, g, b):
    mu = jnp.mean(x, axis=-1, keepdims=True)
    xc = x - mu
    var = jnp.mean(xc * xc, axis=-1, keepdims=True)
    return xc * lax.rsqrt(var + _LN_EPS) * g + b


def _cast_kernel(x_ref, o_ref):
    o_ref[...] = x_ref[...].astype(o_ref.dtype)


def _cast_bf16(x, tm=512):
    m, n = x.shape
    return pl.pallas_call(
        _cast_kernel,
        grid=(m // tm,),
        in_specs=[pl.BlockSpec((tm, n), lambda i: (i, 0))],
        out_specs=pl.BlockSpec((tm, n), lambda i: (i, 0)),
        out_shape=jax.ShapeDtypeStruct((m, n), _BF16),
        compiler_params=_params(("arbitrary",)),
        name="cast_x",
    )(x)


def _linear_kernel(a_ref, w_ref, *rest, epilogue):
    *extras, o_ref = rest
    acc = _dot(a_ref[...], w_ref[...].astype(_BF16))
    epilogue(acc, o_ref, *extras)


def _row_spec(tm, width):
    return pl.BlockSpec((tm, width), lambda i, j: (i, 0))


def _col_spec(tn, blk0=0):
    return pl.BlockSpec((1, tn), lambda i, j: (0, blk0 + j))


def _tile_spec(tm, tn, blk0=0):
    return pl.BlockSpec((tm, tn), lambda i, j: (i, blk0 + j))


def _linear(name, a, w, col0, n_cols, epilogue, extras=(), out_dtype=_BF16, tm=_TM, tn=_TN,
            transposed_out=False):
    m, k = a.shape
    blk0 = col0 // tn
    arrays = [x for x, _ in extras]
    specs = [s for _, s in extras]
    if transposed_out:
        out_spec = pl.BlockSpec((tn, tm), lambda i, j: (j, i))
        out_shape = jax.ShapeDtypeStruct((n_cols, m), out_dtype)
    else:
        out_spec = pl.BlockSpec((tm, tn), lambda i, j: (i, j))
        out_shape = jax.ShapeDtypeStruct((m, n_cols), out_dtype)
    return pl.pallas_call(
        functools.partial(_linear_kernel, epilogue=epilogue),
        grid=(m // tm, n_cols // tn),
        in_specs=[
            pl.BlockSpec((tm, k), lambda i, j: (i, 0)),
            pl.BlockSpec((k, tn), lambda i, j: (0, blk0 + j)),
            *specs,
        ],
        out_specs=out_spec,
        out_shape=out_shape,
        compiler_params=_params(("arbitrary", "arbitrary")),
        name=name,
    )(a, w, *arrays)


def _transpose_epilogue(acc, o_ref):
    o_ref[...] = acc.T.astype(o_ref.dtype)


def _gelu_epilogue(acc, o_ref):
    o_ref[...] = jax.nn.gelu(acc).astype(o_ref.dtype)


def _sigmoid_epilogue(acc, o_ref):
    o_ref[...] = jax.nn.sigmoid(acc).astype(o_ref.dtype)


def _rope_epilogue(acc, o_ref, cos_ref, sup_ref, sdn_ref, scale_ref):
    cos, sup, sdn = cos_ref[...], sup_ref[...], sdn_ref[...]
    for c in range(acc.shape[1] // _LANES):
        sl = slice(c * _LANES, (c + 1) * _LANES)
        t = acc[:, sl]
        r = (t * cos
             + pltpu.roll(t, _ROPE_HALF, 1) * sup
             + pltpu.roll(t, _LANES - _ROPE_HALF, 1) * sdn)
        o_ref[:, sl] = (r * scale_ref[:, sl]).astype(o_ref.dtype)


def _residual_epilogue(acc, o_ref, x_ref):
    o_ref[...] = _ALPHA * x_ref[...] + acc


def _relu2_epilogue(acc, o_ref, b_ref):
    z = jnp.maximum(acc + b_ref[...], 0.0)
    o_ref[...] = (z * z).astype(o_ref.dtype)


def _attn_kernel(qi_tab, ki_tab, lq1_ref, lk1_ref, lq2_ref, lk2_ref, subw_ref,
                 q_ref, k_ref, vt_ref, o_ref, m_ref, l_ref, acc_ref, *, tq, tk, rq, kh, ahead, lambda_init):
    p_idx = pl.program_id(1)
    qi = qi_tab[p_idx]
    ki = ki_tab[p_idx]
    last_k = ((qi + 1) * tq - 1) // tk

    @pl.when(ki == 0)
    def _():
        m_ref[...] = jnp.full(m_ref.shape, -jnp.inf, _F32)
        l_ref[...] = jnp.zeros(l_ref.shape, _F32)
        acc_ref[...] = jnp.zeros(acc_ref.shape, _F32)

    def step(off):
        vt = vt_ref[...]

        def keys_used(r):
            if off is None:
                return tk
            last_q = (r + 1) * rq - 1
            return kh * sum(1 for h in range(tk // kh) if off + h * kh <= last_q)

        def needs_mask(r):
            return off is not None and off + keys_used(r) - 1 > r * rq

        chunks = [(c, r) for c in range(2) for r in range(tq // rq) if keys_used(r) > 0]

        def scores(c, r):
            sl = slice(c * _HEAD_DIM, (c + 1) * _HEAD_DIM)
            q = q_ref[r * rq:(r + 1) * rq, sl]
            parts = [lax.dot_general(k_ref[h * kh:(h + 1) * kh, sl], q, (((1,), (1,)), ((), ())),
                                     preferred_element_type=_F32) for h in range(keys_used(r) // kh)]
            return jnp.concatenate(parts, axis=0)

        def softmax_pv(c, r, st):
            qs = slice(r * rq, (r + 1) * rq)
            nk = keys_used(r)
            if needs_mask(r):
                delta = (lax.broadcasted_iota(jnp.int32, (nk, rq), 0)
                         - lax.broadcasted_iota(jnp.int32, (nk, rq), 1))
                st = jnp.where(delta <= r * rq - off, st, -jnp.inf)
            m_prev = m_ref[c, :, qs]
            m_new = jnp.maximum(m_prev, jnp.max(st, axis=0, keepdims=True))
            alpha = jnp.exp2(m_prev - m_new)
            p = jnp.exp2(st - m_new)
            l_ref[c, :, qs] = alpha * l_ref[c, :, qs] + jnp.sum(p, axis=0, keepdims=True)
            acc_ref[c, :, qs] = alpha * acc_ref[c, :, qs] + _dot(vt[:, :nk], p.astype(_BF16))
            m_ref[c, :, qs] = m_new

        pending = [scores(*ch) for ch in chunks[:ahead]]
        for i, (c, r) in enumerate(chunks):
            if i + ahead < len(chunks):
                pending.append(scores(*chunks[i + ahead]))
            softmax_pv(c, r, pending.pop(0))

    off = ki * tk - qi * tq

    @pl.when(off + tk <= 0)
    def _():
        step(None)

    for diag_off in range(0, tq, tk):
        @pl.when(off == diag_off)
        def _(diag_off=diag_off):
            step(diag_off)

    @pl.when(ki == last_k)
    def _():
        lam = (jnp.exp(jnp.sum(lq1_ref[...] * lk1_ref[...], axis=1, keepdims=True))
               - jnp.exp(jnp.sum(lq2_ref[...] * lk2_ref[...], axis=1, keepdims=True))
               + lambda_init)
        ot = acc_ref[0] / l_ref[0] - lam * (acc_ref[1] / l_ref[1])
        ot = ot * lax.rsqrt(jnp.mean(ot * ot, axis=0, keepdims=True) + _LN_EPS)
        ot = ot * subw_ref[...] * (1.0 - lambda_init)
        o_ref[...] = ot.T.astype(o_ref.dtype)


def _causal_pairs(n_q, tq, tk):
    qs, ks = [], []
    for qi in range(n_q):
        for ki in range(((qi + 1) * tq - 1) // tk + 1):
            qs.append(qi)
            ks.append(ki)
    return np.asarray(qs, np.int32), np.asarray(ks, np.int32)


def _attention(qk, vt, lq1, lk1, lq2, lk2, subw_col, lambda_init, tq=2048, tk=1024, rq=512, kh=512, ahead=2):
    s_len = qk.shape[0]
    qs, ks = _causal_pairs(s_len // tq, tq, tk)
    k_blk0 = _SEG // _V_DIM
    vec = lambda n: pl.BlockSpec((1, n), lambda h, p, qt, kt: (0, 0))
    kern = functools.partial(_attn_kernel, tq=tq, tk=tk, rq=rq, kh=kh, ahead=ahead, lambda_init=lambda_init)
    return pl.pallas_call(
        kern,
        grid_spec=pltpu.PrefetchScalarGridSpec(
            num_scalar_prefetch=2,
            grid=(_HEADS, len(qs)),
            in_specs=[
                vec(_HEAD_DIM), vec(_HEAD_DIM), vec(_HEAD_DIM), vec(_HEAD_DIM),
                pl.BlockSpec((_V_DIM, 1), lambda h, p, qt, kt: (0, 0)),
                pl.BlockSpec((tq, _V_DIM), lambda h, p, qt, kt: (qt[p], h)),
                pl.BlockSpec((tk, _V_DIM), lambda h, p, qt, kt: (kt[p], k_blk0 + h)),
                pl.BlockSpec((_V_DIM, tk), lambda h, p, qt, kt: (h, kt[p])),
            ],
            out_specs=pl.BlockSpec((tq, _V_DIM), lambda h, p, qt, kt: (qt[p], h)),
            scratch_shapes=[
                pltpu.VMEM((2, 1, tq), _F32),
                pltpu.VMEM((2, 1, tq), _F32),
                pltpu.VMEM((2, _V_DIM, tq), _F32),
            ],
        ),
        out_shape=jax.ShapeDtypeStruct((s_len, _HEADS * _V_DIM), _BF16),
        compiler_params=_params(("arbitrary", "arbitrary")),
        name="diff_attn",
    )(jnp.asarray(qs), jnp.asarray(ks), lq1, lk1, lq2, lk2, subw_col, qk, qk, vt)


def _sgu_kernel(u_ref, s_ref, g_ref, b_ref, w_ref, bs_ref, o_ref, *, tm):
    gd = _SEG // _SGU_GROUPS
    sn = _layer_norm_rows(s_ref[...].astype(_F32), g_ref[...], b_ref[...]).astype(_BF16)
    row = lax.broadcasted_iota(jnp.int32, (_CHUNK, _CHUNK), 0)
    col = lax.broadcasted_iota(jnp.int32, (_CHUNK, _CHUNK), 1)
    for g in range(_SGU_GROUPS):
        w = jnp.where(row >= col, w_ref[g], 0.0).astype(_BF16)
        bias = bs_ref[:, g:g + 1]
        cs = slice(g * gd, (g + 1) * gd)
        for c in range(tm // _CHUNK):
            rs = slice(c * _CHUNK, (c + 1) * _CHUNK)
            mixed = _dot(w, sn[rs, cs]) + bias
            o_ref[rs, cs] = (u_ref[rs, cs].astype(_F32) * mixed).astype(o_ref.dtype)


def _sgu(us, ln_g, ln_b, w_s, b_s_t, tm=512):
    s_len = us.shape[0]
    return pl.pallas_call(
        functools.partial(_sgu_kernel, tm=tm),
        grid=(s_len // tm,),
        in_specs=[
            pl.BlockSpec((tm, _SEG), lambda i: (i, 0)),
            pl.BlockSpec((tm, _SEG), lambda i: (i, 1)),
            pl.BlockSpec((1, _SEG), lambda i: (0, 0)),
            pl.BlockSpec((1, _SEG), lambda i: (0, 0)),
            pl.BlockSpec((_SGU_GROUPS, _CHUNK, _CHUNK), lambda i: (0, 0, 0)),
            pl.BlockSpec((_CHUNK, _SGU_GROUPS), lambda i: (0, 0)),
        ],
        out_specs=pl.BlockSpec((tm, _SEG), lambda i: (i, 0)),
        out_shape=jax.ShapeDtypeStruct((s_len, _SEG), _BF16),
        compiler_params=_params(("arbitrary",)),
        name="sgu",
    )(us, us, ln_g, ln_b, w_s, b_s_t)


def _merge_kernel(ya_ref, yb_ref, wa_ref, wb_ref, ga_ref, gb_ref, o_ref):
    a = _dot(ya_ref[...], wa_ref[...].astype(_BF16))
    b = _dot(yb_ref[...], wb_ref[...].astype(_BF16))
    o_ref[...] = (ga_ref[...].astype(_F32) * a + gb_ref[...].astype(_F32) * b).astype(o_ref.dtype)


def _merge(y_a, y_b, w_a, w_b, gates, tm=_TM, tn=_TN):
    m, k = y_a.shape
    n = w_a.shape[1]
    return pl.pallas_call(
        _merge_kernel,
        grid=(m // tm, n // tn),
        in_specs=[
            pl.BlockSpec((tm, k), lambda i, j: (i, 0)),
            pl.BlockSpec((tm, k), lambda i, j: (i, 0)),
            pl.BlockSpec((k, tn), lambda i, j: (0, j)),
            pl.BlockSpec((k, tn), lambda i, j: (0, j)),
            _tile_spec(tm, tn),
            _tile_spec(tm, tn, n // tn),
        ],
        out_specs=pl.BlockSpec((tm, tn), lambda i, j: (i, j)),
        out_shape=jax.ShapeDtypeStruct((m, n), _BF16),
        compiler_params=_params(("arbitrary", "arbitrary")),
        name="merge",
    )(y_a, y_b, w_a, w_b, gates, gates)


def _ln_bf16_kernel(x_ref, g_ref, b_ref, o_ref):
    o_ref[...] = _layer_norm_rows(x_ref[...], g_ref[...], b_ref[...]).astype(o_ref.dtype)


def _ln_bf16(x, g, b, tm=256):
    m, n = x.shape
    row = pl.BlockSpec((tm, n), lambda i: (i, 0))
    vec = pl.BlockSpec((1, n), lambda i: (0, 0))
    return pl.pallas_call(
        _ln_bf16_kernel,
        grid=(m // tm,),
        in_specs=[row, vec, vec],
        out_specs=row,
        out_shape=jax.ShapeDtypeStruct((m, n), _BF16),
        compiler_params=_params(("arbitrary",)),
        name="ln1",
    )(x, g, b)


def _final_kernel(pre_ref, ff_ref, g1_ref, b1_ref, bo_ref, g2_ref, b2_ref, o_ref):
    h1 = _layer_norm_rows(pre_ref[...], g1_ref[...], b1_ref[...])
    o_ref[...] = _layer_norm_rows(_ALPHA * h1 + (ff_ref[...] + bo_ref[...]), g2_ref[...], b2_ref[...])


def _final(pre1, ff, g1, b1, b_out, g2, b2, tm=256):
    m, n = pre1.shape
    row = pl.BlockSpec((tm, n), lambda i: (i, 0))
    vec = pl.BlockSpec((1, n), lambda i: (0, 0))
    return pl.pallas_call(
        _final_kernel,
        grid=(m // tm,),
        in_specs=[row, row, vec, vec, vec, vec, vec],
        out_specs=row,
        out_shape=jax.ShapeDtypeStruct((m, n), _F32),
        compiler_params=_params(("arbitrary",)),
        name="final_ln",
    )(pre1, ff, g1, b1, b_out, g2, b2)


def _mlp_out_kernel(z_ref, w_ref, o_ref):
    kk = pl.program_id(2)

    @pl.when(kk == 0)
    def _():
        o_ref[...] = _dot(z_ref[...], w_ref[...].astype(_BF16))

    @pl.when(kk > 0)
    def _():
        o_ref[...] += _dot(z_ref[...], w_ref[...].astype(_BF16))


def _mlp_out(z, w, tm=2048, tn=1024, tk=1024):
    m, k = z.shape
    n = w.shape[1]
    return pl.pallas_call(
        _mlp_out_kernel,
        grid=(m // tm, n // tn, k // tk),
        in_specs=[
            pl.BlockSpec((tm, tk), lambda i, j, kk: (i, kk)),
            pl.BlockSpec((tk, tn), lambda i, j, kk: (kk, j)),
        ],
        out_specs=pl.BlockSpec((tm, tn), lambda i, j, kk: (i, j)),
        out_shape=jax.ShapeDtypeStruct((m, n), _F32),
        compiler_params=_params(("arbitrary", "arbitrary", "arbitrary")),
        name="mlp_out",
    )(z, w)


def _rope_tables(s_len):
    inv_freq = _ROPE_THETA ** (-jnp.arange(0, _ROPE_DIM, 2, dtype=_F32) / _ROPE_DIM)
    ang = jnp.arange(s_len).astype(_F32)[:, None] * inv_freq[None, :]
    cos, sin = jnp.cos(ang), jnp.sin(ang)
    ones = jnp.ones((s_len, _LANES - _ROPE_DIM), _F32)
    zeros_h = jnp.zeros((s_len, _ROPE_HALF), _F32)
    zeros_r = jnp.zeros((s_len, _LANES - _ROPE_DIM), _F32)
    cos_t = jnp.concatenate([cos, cos, ones], axis=1)
    sin_up = jnp.concatenate([zeros_h, sin, zeros_r], axis=1)
    sin_dn = jnp.concatenate([-sin, zeros_h, zeros_r], axis=1)
    return cos_t, sin_up, sin_dn


def kernel(x, w_in, lambda_q1, lambda_k1, lambda_q2, lambda_k2, subln_w, sgu_ln_g, sgu_ln_b,
           w_spatial, b_spatial, w_proj_attn, w_proj_sgu, w_out, ln1_g, ln1_b, w_mlp_in, b_mlp_in,
           w_mlp_out, b_mlp_out, ln2_g, ln2_b):
    bsz, s_len, d = x.shape
    assert bsz == 1 and w_in.shape[0] == _DEPTH == 1
    h = x.reshape(s_len, d)
    l = 0
    lambda_init = 0.8 - 0.6 * math.exp(-0.3 * l)
    cos_t, sin_up, sin_dn = _rope_tables(s_len)
    q_scale = _HEAD_DIM ** -0.5 * math.log2(math.e)
    qk_scale = jnp.concatenate([jnp.full((1, _SEG), q_scale, _F32), jnp.ones((1, _SEG), _F32)], axis=1)

    hb = _cast_bf16(h)
    w_in_l = w_in[l]
    qk = _linear("proj_qk", hb, w_in_l, 0, 2 * _SEG, _rope_epilogue,
                 extras=[(cos_t, _row_spec(_TM, _LANES)), (sin_up, _row_spec(_TM, _LANES)),
                         (sin_dn, _row_spec(_TM, _LANES)), (qk_scale, _col_spec(_TN))])
    vt = _linear("proj_v", hb, w_in_l, 2 * _SEG, _SEG, _transpose_epilogue, transposed_out=True)
    us = _linear("proj_us", hb, w_in_l, 3 * _SEG, 2 * _SEG, _gelu_epilogue)
    gates = _linear("proj_gates", hb, w_in_l, 5 * _SEG, 2 * d, _sigmoid_epilogue)

    y_a = _attention(qk, vt, lambda_q1[l][None], lambda_k1[l][None], lambda_q2[l][None],
                     lambda_k2[l][None], subln_w[l][:, None], lambda_init)
    y_b = _sgu(us, sgu_ln_g[l][None], sgu_ln_b[l][None], w_spatial[l], b_spatial[l].T)
    merged = _merge(y_a, y_b, w_proj_attn[l], w_proj_sgu[l], gates)
    pre1 = _linear("outproj", merged, w_out[l], 0, d, _residual_epilogue,
                   extras=[(h, _tile_spec(_TM, _TN))], out_dtype=_F32)
    h1b = _ln_bf16(pre1, ln1_g[l][None], ln1_b[l][None])
    z = _linear("mlp_in", h1b, w_mlp_in[l], 0, w_mlp_in.shape[2], _relu2_epilogue,
                extras=[(b_mlp_in[l][None], _col_spec(_TN))])
    ff = _mlp_out(z, w_mlp_out[l])
    out = _final(pre1, ff, ln1_g[l][None], ln1_b[l][None], b_mlp_out[l][None],
                 ln2_g[l][None], ln2_b[l][None])
    return out.reshape(bsz, s_len, d)
```

```python
import functools
import math

import numpy as np
import jax
import jax.numpy as jnp
from jax import lax
from jax.experimental import pallas as pl
from jax.experimental.pallas import tpu as pltpu

_F32 = jnp.float32
_BF16 = jnp.bfloat16

_LANES = 128
_VMEM_LIMIT_BYTES = 60 * 1024 * 1024

_HEADS = 8
_HEAD_DIM = 128
_V_DIM = 2 * _HEAD_DIM
_ROPE_DIM = _HEAD_DIM // 4
_ROPE_HALF = _ROPE_DIM // 2
_ROPE_THETA = 500000.0
_SGU_GROUPS = 8
_CHUNK = 128
_DEPTH = 1
_ALPHA = (2.0 * _DEPTH) ** 0.25
_LN_EPS = 1e-5
_SEG = 2048

_TM = 1024
_TN = 512
_ROW_SPLITS = 4


def _params(sem):
    return pltpu.CompilerParams(dimension_semantics=sem, vmem_limit_bytes=_VMEM_LIMIT_BYTES)


def _dot(a, b):
    return jnp.dot(a, b, preferred_element_type=_F32)


def _layer_norm_rows(x, g, b):
    mu = jnp.mean(x, axis=-1, keepdims=True)
    xc = x - mu
    var = jnp.mean(xc * xc, axis=-1, keepdims=True)
    return xc * lax.rsqrt(var + _LN_EPS) * g + b


def _cast_kernel(x_ref, o_ref):
    o_ref[...] = x_ref[...].astype(o_ref.dtype)


def _cast_bf16(x, tm=512):
    m, n = x.shape
    return pl.pallas_call(
        _cast_kernel,
        grid=(m // tm,),
        in_specs=[pl.BlockSpec((tm, n), lambda i: (i, 0))],
        out_specs=pl.BlockSpec((tm, n), lambda i: (i, 0)),
        out_shape=jax.ShapeDtypeStruct((m, n), _BF16),
        compiler_params=_params(("arbitrary",)),
        name="cast_x",
    )(x)


def _linear_kernel(a_ref, w_ref, *rest, epilogue, row_splits):
    *extras, o_ref = rest
    wb = w_ref[...].astype(_BF16)
    slab = a_ref.shape[0] // row_splits
    for s in range(row_splits):
        rows = slice(s * slab, (s + 1) * slab)
        epilogue(_dot(a_ref[rows, :], wb), rows, o_ref, *extras)


def _row_spec(tm, width):
    return pl.BlockSpec((tm, width), lambda i, j: (i, 0))


def _col_spec(tn, blk0=0):
    return pl.BlockSpec((1, tn), lambda i, j: (0, blk0 + j))


def _tile_spec(tm, tn, blk0=0):
    return pl.BlockSpec((tm, tn), lambda i, j: (i, blk0 + j))


def _linear(name, a, w, col0, n_cols, epilogue, extras=(), out_dtype=_BF16, tm=_TM, tn=_TN,
            transposed_out=False, row_splits=_ROW_SPLITS):
    m, k = a.shape
    blk0 = col0 // tn
    arrays = [x for x, _ in extras]
    specs = [s for _, s in extras]
    if transposed_out:
        out_spec = pl.BlockSpec((tn, tm), lambda i, j: (j, i))
        out_shape = jax.ShapeDtypeStruct((n_cols, m), out_dtype)
    else:
        out_spec = pl.BlockSpec((tm, tn), lambda i, j: (i, j))
        out_shape = jax.ShapeDtypeStruct((m, n_cols), out_dtype)
    return pl.pallas_call(
        functools.partial(_linear_kernel, epilogue=epilogue, row_splits=row_splits),
        grid=(m // tm, n_cols // tn),
        in_specs=[
            pl.BlockSpec((tm, k), lambda i, j: (i, 0)),
            pl.BlockSpec((k, tn), lambda i, j: (0, blk0 + j)),
            *specs,
        ],
        out_specs=out_spec,
        out_shape=out_shape,
        compiler_params=_params(("arbitrary", "arbitrary")),
        name=name,
    )(a, w, *arrays)


def _transpose_epilogue(acc, rows, o_ref):
    o_ref[:, rows] = acc.T.astype(o_ref.dtype)


def _gelu_epilogue(acc, rows, o_ref):
    o_ref[rows, :] = jax.nn.gelu(acc).astype(o_ref.dtype)


def _sigmoid_epilogue(acc, rows, o_ref):
    o_ref[rows, :] = jax.nn.sigmoid(acc).astype(o_ref.dtype)


def _rope_epilogue(acc, rows, o_ref, cos_ref, sup_ref, sdn_ref, scale_ref):
    cos, sup, sdn = cos_ref[rows, :], sup_ref[rows, :], sdn_ref[rows, :]
    for c in range(acc.shape[1] // _LANES):
        sl = slice(c * _LANES, (c + 1) * _LANES)
        t = acc[:, sl]
        r = (t * cos
             + pltpu.roll(t, _ROPE_HALF, 1) * sup
             + pltpu.roll(t, _LANES - _ROPE_HALF, 1) * sdn)
        o_ref[rows, sl] = (r * scale_ref[:, sl]).astype(o_ref.dtype)


def _residual_epilogue(acc, rows, o_ref, x_ref):
    o_ref[rows, :] = _ALPHA * x_ref[rows, :] + acc


def _relu2_epilogue(acc, rows, o_ref, b_ref):
    z = jnp.maximum(acc + b_ref[...], 0.0)
    o_ref[rows, :] = (z * z).astype(o_ref.dtype)


def _attn_kernel(qi_tab, ki_tab, lq1_ref, lk1_ref, lq2_ref, lk2_ref, subw_ref,
                 q_ref, k_ref, vt_ref, o_ref, m_ref, l_ref, acc_ref, *,
                 tq, tk, rq, ks, kh, ahead, lambda_init):
    p_idx = pl.program_id(1)
    qi = qi_tab[p_idx]
    ki = ki_tab[p_idx]
    last_k = ((qi + 1) * tq - 1) // tk

    @pl.when(ki == 0)
    def _():
        m_ref[...] = jnp.full(m_ref.shape, -jnp.inf, _F32)
        l_ref[...] = jnp.zeros(l_ref.shape, _F32)
        acc_ref[...] = jnp.zeros(acc_ref.shape, _F32)

    def step(off):

        def span_off(b):
            return None if off is None else off + b * ks

        def keys_used(r, b):
            if off is None:
                return ks
            last_q = (r + 1) * rq - 1
            return kh * sum(1 for h in range(ks // kh) if span_off(b) + h * kh <= last_q)

        def needs_mask(r, b):
            return off is not None and span_off(b) + keys_used(r, b) - 1 > r * rq

        chunks = [(c, r, b) for b in range(tk // ks) for c in range(2) for r in range(tq // rq)
                  if keys_used(r, b) > 0]

        def scores(c, r, b):
            sl = slice(c * _HEAD_DIM, (c + 1) * _HEAD_DIM)
            q = q_ref[r * rq:(r + 1) * rq, sl]
            parts = [lax.dot_general(k_ref[b * ks + h * kh:b * ks + (h + 1) * kh, sl], q,
                                     (((1,), (1,)), ((), ())), preferred_element_type=_F32)
                     for h in range(keys_used(r, b) // kh)]
            return jnp.concatenate(parts, axis=0)

        def softmax_pv(c, r, b, st):
            qs = slice(r * rq, (r + 1) * rq)
            nk = keys_used(r, b)
            if needs_mask(r, b):
                delta = (lax.broadcasted_iota(jnp.int32, (nk, rq), 0)
                         - lax.broadcasted_iota(jnp.int32, (nk, rq), 1))
                st = jnp.where(delta <= r * rq - span_off(b), st, -jnp.inf)
            m_prev = m_ref[c, :, qs]
            m_new = jnp.maximum(m_prev, jnp.max(st, axis=0, keepdims=True))
            alpha = jnp.exp2(m_prev - m_new)
            p = jnp.exp2(st - m_new)
            l_ref[c, :, qs] = alpha * l_ref[c, :, qs] + jnp.sum(p, axis=0, keepdims=True)
            acc_ref[c, :, qs] = (alpha * acc_ref[c, :, qs]
                                 + _dot(vt_ref[:, b * ks:b * ks + nk], p.astype(_BF16)))
            m_ref[c, :, qs] = m_new

        pending = [scores(*ch) for ch in chunks[:ahead]]
        for i, ch in enumerate(chunks):
            if i + ahead < len(chunks):
                pending.append(scores(*chunks[i + ahead]))
            softmax_pv(*ch, pending.pop(0))

    off = ki * tk - qi * tq

    @pl.when(off + tk <= 0)
    def _():
        step(None)

    for diag_off in range(0, tq, tk):
        @pl.when(off == diag_off)
        def _(diag_off=diag_off):
            step(diag_off)

    @pl.when(ki == last_k)
    def _():
        lam = (jnp.exp(jnp.sum(lq1_ref[...] * lk1_ref[...], axis=1, keepdims=True))
               - jnp.exp(jnp.sum(lq2_ref[...] * lk2_ref[...], axis=1, keepdims=True))
               + lambda_init)
        ot = acc_ref[0] / l_ref[0] - lam * (acc_ref[1] / l_ref[1])
        ot = ot * lax.rsqrt(jnp.mean(ot * ot, axis=0, keepdims=True) + _LN_EPS)
        ot = ot * subw_ref[...] * (1.0 - lambda_init)
        o_ref[...] = ot.T.astype(o_ref.dtype)


def _causal_pairs(n_q, tq, tk):
    qs, ks = [], []
    for qi in range(n_q):
        for ki in range(((qi + 1) * tq - 1) // tk + 1):
            qs.append(qi)
            ks.append(ki)
    return np.asarray(qs, np.int32), np.asarray(ks, np.int32)


def _attention(qk, vt, lq1, lk1, lq2, lk2, subw_col, lambda_init,
               tq=2048, tk=2048, rq=512, ks=1024, kh=512, ahead=2):
    s_len = qk.shape[0]
    q_blocks, k_blocks = _causal_pairs(s_len // tq, tq, tk)
    k_blk0 = _SEG // _V_DIM
    vec = lambda n: pl.BlockSpec((1, n), lambda h, p, qt, kt: (0, 0))
    kern = functools.partial(_attn_kernel, tq=tq, tk=tk, rq=rq, ks=ks, kh=kh, ahead=ahead,
                             lambda_init=lambda_init)
    return pl.pallas_call(
        kern,
        grid_spec=pltpu.PrefetchScalarGridSpec(
            num_scalar_prefetch=2,
            grid=(_HEADS, len(q_blocks)),
            in_specs=[
                vec(_HEAD_DIM), vec(_HEAD_DIM), vec(_HEAD_DIM), vec(_HEAD_DIM),
                pl.BlockSpec((_V_DIM, 1), lambda h, p, qt, kt: (0, 0)),
                pl.BlockSpec((tq, _V_DIM), lambda h, p, qt, kt: (qt[p], h)),
                pl.BlockSpec((tk, _V_DIM), lambda h, p, qt, kt: (kt[p], k_blk0 + h)),
                pl.BlockSpec((_V_DIM, tk), lambda h, p, qt, kt: (h, kt[p])),
            ],
            out_specs=pl.BlockSpec((tq, _V_DIM), lambda h, p, qt, kt: (qt[p], h)),
            scratch_shapes=[
                pltpu.VMEM((2, 1, tq), _F32),
                pltpu.VMEM((2, 1, tq), _F32),
                pltpu.VMEM((2, _V_DIM, tq), _F32),
            ],
        ),
        out_shape=jax.ShapeDtypeStruct((s_len, _HEADS * _V_DIM), _BF16),
        compiler_params=_params(("arbitrary", "arbitrary")),
        name="diff_attn",
    )(jnp.asarray(q_blocks), jnp.asarray(k_blocks), lq1, lk1, lq2, lk2, subw_col, qk, qk, vt)


def _sgu_kernel(u_ref, s_ref, g_ref, b_ref, w_ref, bs_ref, o_ref, *, tm):
    gd = _SEG // _SGU_GROUPS
    sn = _layer_norm_rows(s_ref[...].astype(_F32), g_ref[...], b_ref[...]).astype(_BF16)
    row = lax.broadcasted_iota(jnp.int32, (_CHUNK, _CHUNK), 0)
    col = lax.broadcasted_iota(jnp.int32, (_CHUNK, _CHUNK), 1)
    for g in range(_SGU_GROUPS):
        w = jnp.where(row >= col, w_ref[g], 0.0).astype(_BF16)
        bias = bs_ref[:, g:g + 1]
        cs = slice(g * gd, (g + 1) * gd)
        for c in range(tm // _CHUNK):
            rs = slice(c * _CHUNK, (c + 1) * _CHUNK)
            mixed = _dot(w, sn[rs, cs]) + bias
            o_ref[rs, cs] = (u_ref[rs, cs].astype(_F32) * mixed).astype(o_ref.dtype)


def _sgu(us, ln_g, ln_b, w_s, b_s_t, tm=512):
    s_len = us.shape[0]
    return pl.pallas_call(
        functools.partial(_sgu_kernel, tm=tm),
        grid=(s_len // tm,),
        in_specs=[
            pl.BlockSpec((tm, _SEG), lambda i: (i, 0)),
            pl.BlockSpec((tm, _SEG), lambda i: (i, 1)),
            pl.BlockSpec((1, _SEG), lambda i: (0, 0)),
            pl.BlockSpec((1, _SEG), lambda i: (0, 0)),
            pl.BlockSpec((_SGU_GROUPS, _CHUNK, _CHUNK), lambda i: (0, 0, 0)),
            pl.BlockSpec((_CHUNK, _SGU_GROUPS), lambda i: (0, 0)),
        ],
        out_specs=pl.BlockSpec((tm, _SEG), lambda i: (i, 0)),
        out_shape=jax.ShapeDtypeStruct((s_len, _SEG), _BF16),
        compiler_params=_params(("arbitrary",)),
        name="sgu",
    )(us, us, ln_g, ln_b, w_s, b_s_t)


def _merge_kernel(ya_ref, yb_ref, wa_ref, wb_ref, ga_ref, gb_ref, o_ref):
    wa = wa_ref[...].astype(_BF16)
    wb = wb_ref[...].astype(_BF16)
    slab = ya_ref.shape[0] // _ROW_SPLITS
    for s in range(_ROW_SPLITS):
        rows = slice(s * slab, (s + 1) * slab)
        a = _dot(ya_ref[rows, :], wa)
        b = _dot(yb_ref[rows, :], wb)
        o_ref[rows, :] = (ga_ref[rows, :].astype(_F32) * a
                          + gb_ref[rows, :].astype(_F32) * b).astype(o_ref.dtype)


def _merge(y_a, y_b, w_a, w_b, gates, tm=_TM, tn=_TN):
    m, k = y_a.shape
    n = w_a.shape[1]
    return pl.pallas_call(
        _merge_kernel,
        grid=(m // tm, n // tn),
        in_specs=[
            pl.BlockSpec((tm, k), lambda i, j: (i, 0)),
            pl.BlockSpec((tm, k), lambda i, j: (i, 0)),
            pl.BlockSpec((k, tn), lambda i, j: (0, j)),
            pl.BlockSpec((k, tn), lambda i, j: (0, j)),
            _tile_spec(tm, tn),
            _tile_spec(tm, tn, n // tn),
        ],
        out_specs=pl.BlockSpec((tm, tn), lambda i, j: (i, j)),
        out_shape=jax.ShapeDtypeStruct((m, n), _BF16),
        compiler_params=_params(("arbitrary", "arbitrary")),
        name="merge",
    )(y_a, y_b, w_a, w_b, gates, gates)


def _ln_bf16_kernel(x_ref, g_ref, b_ref, o_ref):
    o_ref[...] = _layer_norm_rows(x_ref[...], g_ref[...], b_ref[...]).astype(o_ref.dtype)


def _ln_bf16(x, g, b, tm=256):
    m, n = x.shape
    row = pl.BlockSpec((tm, n), lambda i: (i, 0))
    vec = pl.BlockSpec((1, n), lambda i: (0, 0))
    return pl.pallas_call(
        _ln_bf16_kernel,
        grid=(m // tm,),
        in_specs=[row, vec, vec],
        out_specs=row,
        out_shape=jax.ShapeDtypeStruct((m, n), _BF16),
        compiler_params=_params(("arbitrary",)),
        name="ln1",
    )(x, g, b)


def _final_kernel(pre_ref, ff_ref, g1_ref, b1_ref, bo_ref, g2_ref, b2_ref, o_ref):
    h1 = _layer_norm_rows(pre_ref[...], g1_ref[...], b1_ref[...])
    o_ref[...] = _layer_norm_rows(_ALPHA * h1 + (ff_ref[...] + bo_ref[...]), g2_ref[...], b2_ref[...])


def _final(pre1, ff, g1, b1, b_out, g2, b2, tm=256):
    m, n = pre1.shape
    row = pl.BlockSpec((tm, n), lambda i: (i, 0))
    vec = pl.BlockSpec((1, n), lambda i: (0, 0))
    return pl.pallas_call(
        _final_kernel,
        grid=(m // tm,),
        in_specs=[row, row, vec, vec, vec, vec, vec],
        out_specs=row,
        out_shape=jax.ShapeDtypeStruct((m, n), _F32),
        compiler_params=_params(("arbitrary",)),
        name="final_ln",
    )(pre1, ff, g1, b1, b_out, g2, b2)


def _mlp_out_kernel(z_ref, w_ref, o_ref):
    kk = pl.program_id(2)
    slab = z_ref.shape[0] // _ROW_SPLITS

    def accumulate(first):
        wb = w_ref[...].astype(_BF16)
        for s in range(_ROW_SPLITS):
            rows = slice(s * slab, (s + 1) * slab)
            part = _dot(z_ref[rows, :], wb)
            o_ref[rows, :] = part if first else o_ref[rows, :] + part

    @pl.when(kk == 0)
    def _():
        accumulate(True)

    @pl.when(kk > 0)
    def _():
        accumulate(False)


def _mlp_out(z, w, tm=2048, tn=1024, tk=1024):
    m, k = z.shape
    n = w.shape[1]
    return pl.pallas_call(
        _mlp_out_kernel,
        grid=(m // tm, n // tn, k // tk),
        in_specs=[
            pl.BlockSpec((tm, tk), lambda i, j, kk: (i, kk)),
            pl.BlockSpec((tk, tn), lambda i, j, kk: (kk, j)),
        ],
        out_specs=pl.BlockSpec((tm, tn), lambda i, j, kk: (i, j)),
        out_shape=jax.ShapeDtypeStruct((m, n), _F32),
        compiler_params=_params(("arbitrary", "arbitrary", "arbitrary")),
        name="mlp_out",
    )(z, w)


def _rope_tables(s_len):
    inv_freq = _ROPE_THETA ** (-jnp.arange(0, _ROPE_DIM, 2, dtype=_F32) / _ROPE_DIM)
    ang = jnp.arange(s_len).astype(_F32)[:, None] * inv_freq[None, :]
    cos, sin = jnp.cos(ang), jnp.sin(ang)
    ones = jnp.ones((s_len, _LANES - _ROPE_DIM), _F32)
    zeros_h = jnp.zeros((s_len, _ROPE_HALF), _F32)
    zeros_r = jnp.zeros((s_len, _LANES - _ROPE_DIM), _F32)
    cos_t = jnp.concatenate([cos, cos, ones], axis=1)
    sin_up = jnp.concatenate([zeros_h, sin, zeros_r], axis=1)
    sin_dn = jnp.concatenate([-sin, zeros_h, zeros_r], axis=1)
    return cos_t, sin_up, sin_dn


def kernel(x, w_in, lambda_q1, lambda_k1, lambda_q2, lambda_k2, subln_w, sgu_ln_g, sgu_ln_b,
           w_spatial, b_spatial, w_proj_attn, w_proj_sgu, w_out, ln1_g, ln1_b, w_mlp_in, b_mlp_in,
           w_mlp_out, b_mlp_out, ln2_g, ln2_b):
    bsz, s_len, d = x.shape
    assert bsz == 1 and w_in.shape[0] == _DEPTH == 1
    h = x.reshape(s_len, d)
    l = 0
    lambda_init = 0.8 - 0.6 * math.exp(-0.3 * l)
    cos_t, sin_up, sin_dn = _rope_tables(s_len)
    q_scale = _HEAD_DIM ** -0.5 * math.log2(math.e)
    qk_scale = jnp.concatenate([jnp.full((1, _SEG), q_scale, _F32), jnp.ones((1, _SEG), _F32)], axis=1)

    hb = _cast_bf16(h)
    w_in_l = w_in[l]
    qk = _linear("proj_qk", hb, w_in_l, 0, 2 * _SEG, _rope_epilogue,
                 extras=[(cos_t, _row_spec(_TM, _LANES)), (sin_up, _row_spec(_TM, _LANES)),
                         (sin_dn, _row_spec(_TM, _LANES)), (qk_scale, _col_spec(_TN))])
    vt = _linear("proj_v", hb, w_in_l, 2 * _SEG, _SEG, _transpose_epilogue, transposed_out=True)
    us = _linear("proj_us", hb, w_in_l, 3 * _SEG, 2 * _SEG, _gelu_epilogue)
    gates = _linear("proj_gates", hb, w_in_l, 5 * _SEG, 2 * d, _sigmoid_epilogue)

    y_a = _attention(qk, vt, lambda_q1[l][None], lambda_k1[l][None], lambda_q2[l][None],
                     lambda_k2[l][None], subln_w[l][:, None], lambda_init)
    y_b = _sgu(us, sgu_ln_g[l][None], sgu_ln_b[l][None], w_spatial[l], b_spatial[l].T)
    merged = _merge(y_a, y_b, w_proj_attn[l], w_proj_sgu[l], gates)
    pre1 = _linear("outproj", merged, w_out[l], 0, d, _residual_epilogue,
                   extras=[(h, _tile_spec(_TM, _TN))], out_dtype=_F32)
    h1b = _ln_bf16(pre1, ln1_g[l][None], ln1_b[l][None])
    z = _linear("mlp_in", h1b, w_mlp_in[l], 0, w_mlp_in.shape[2], _relu2_epilogue,
                extras=[(b_mlp_in[l][None], _col_spec(_TN))])
    ff = _mlp_out(z, w_mlp_out[l])
    out = _final(pre1, ff, ln1_g[l][None], ln1_b[l][None], b_mlp_out[l][None],
                 ln2_g[l][None], ln2_b[l][None])
    return out.reshape(bsz, s_len, d)
```

```python
import functools
import math

import numpy as np
import jax
import jax.numpy as jnp
from jax import lax
from jax.experimental import pallas as pl
from jax.experimental.pallas import tpu as pltpu

_F32 = jnp.float32
_BF16 = jnp.bfloat16

_LANES = 128
_VMEM_LIMIT_BYTES = 60 * 1024 * 1024

_HEADS = 8
_HEAD_DIM = 128
_V_DIM = 2 * _HEAD_DIM
_ROPE_DIM = _HEAD_DIM // 4
_ROPE_HALF = _ROPE_DIM // 2
_ROPE_THETA = 500000.0
_SGU_GROUPS = 8
_CHUNK = 128
_DEPTH = 1
_ALPHA = (2.0 * _DEPTH) ** 0.25
_LN_EPS = 1e-5
_SEG = 2048

_TM = 1024
_TN = 512
_ROW_SPLITS = 4


def _params(sem):
    return pltpu.CompilerParams(dimension_semantics=sem, vmem_limit_bytes=_VMEM_LIMIT_BYTES)


def _dot(a, b):
    return jnp.dot(a, b, preferred_element_type=_F32)


def _layer_norm_rows(x, g, b):
    mu = jnp.mean(x, axis=-1, keepdims=True)
    xc = x - mu
    var = jnp.mean(xc * xc, axis=-1, keepdims=True)
    return xc * lax.rsqrt(var + _LN_EPS) * g + b


def _cast_kernel(x_ref, o_ref):
    o_ref[...] = x_ref[...].astype(o_ref.dtype)


def _cast_bf16(x, tm=512):
    m, n = x.shape
    return pl.pallas_call(
        _cast_kernel,
        grid=(m // tm,),
        in_specs=[pl.BlockSpec((tm, n), lambda i: (i, 0))],
        out_specs=pl.BlockSpec((tm, n), lambda i: (i, 0)),
        out_shape=jax.ShapeDtypeStruct((m, n), _BF16),
        compiler_params=_params(("arbitrary",)),
        name="cast_x",
    )(x)


def _linear_kernel(a_ref, w_ref, *rest, epilogue, row_splits):
    *extras, o_ref = rest
    wb = w_ref[...].astype(_BF16)
    slab = a_ref.shape[0] // row_splits
    for s in range(row_splits):
        rows = slice(s * slab, (s + 1) * slab)
        epilogue(_dot(a_ref[rows, :], wb), rows, o_ref, *extras)


def _row_spec(tm, width):
    return pl.BlockSpec((tm, width), lambda i, j: (i, 0))


def _col_spec(tn, blk0=0):
    return pl.BlockSpec((1, tn), lambda i, j: (0, blk0 + j))


def _tile_spec(tm, tn, blk0=0):
    return pl.BlockSpec((tm, tn), lambda i, j: (i, blk0 + j))


def _linear(name, a, w, col0, n_cols, epilogue, extras=(), out_dtype=_BF16, tm=_TM, tn=_TN,
            transposed_out=False, row_splits=1):
    m, k = a.shape
    blk0 = col0 // tn
    arrays = [x for x, _ in extras]
    specs = [s for _, s in extras]
    if transposed_out:
        out_spec = pl.BlockSpec((tn, tm), lambda i, j: (j, i))
        out_shape = jax.ShapeDtypeStruct((n_cols, m), out_dtype)
    else:
        out_spec = pl.BlockSpec((tm, tn), lambda i, j: (i, j))
        out_shape = jax.ShapeDtypeStruct((m, n_cols), out_dtype)
    return pl.pallas_call(
        functools.partial(_linear_kernel, epilogue=epilogue, row_splits=row_splits),
        grid=(m // tm, n_cols // tn),
        in_specs=[
            pl.BlockSpec((tm, k), lambda i, j: (i, 0)),
            pl.BlockSpec((k, tn), lambda i, j: (0, blk0 + j)),
            *specs,
        ],
        out_specs=out_spec,
        out_shape=out_shape,
        compiler_params=_params(("arbitrary", "arbitrary")),
        name=name,
    )(a, w, *arrays)


def _transpose_epilogue(acc, rows, o_ref):
    o_ref[:, rows] = acc.T.astype(o_ref.dtype)


def _gelu_epilogue(acc, rows, o_ref):
    o_ref[rows, :] = jax.nn.gelu(acc).astype(o_ref.dtype)


def _sigmoid_epilogue(acc, rows, o_ref):
    o_ref[rows, :] = jax.nn.sigmoid(acc).astype(o_ref.dtype)


def _rope_epilogue(acc, rows, o_ref, cos_ref, sup_ref, sdn_ref, scale_ref):
    cos, sup, sdn = cos_ref[rows, :], sup_ref[rows, :], sdn_ref[rows, :]
    for c in range(acc.shape[1] // _LANES):
        sl = slice(c * _LANES, (c + 1) * _LANES)
        t = acc[:, sl]
        r = (t * cos
             + pltpu.roll(t, _ROPE_HALF, 1) * sup
             + pltpu.roll(t, _LANES - _ROPE_HALF, 1) * sdn)
        o_ref[rows, sl] = (r * scale_ref[:, sl]).astype(o_ref.dtype)


def _residual_epilogue(acc, rows, o_ref, x_ref):
    o_ref[rows, :] = _ALPHA * x_ref[rows, :] + acc


def _relu2_epilogue(acc, rows, o_ref, b_ref):
    z = jnp.maximum(acc + b_ref[...], 0.0)
    o_ref[rows, :] = (z * z).astype(o_ref.dtype)


def _attn_kernel(qi_tab, ki_tab, lq1_ref, lk1_ref, lq2_ref, lk2_ref, subw_ref,
                 q_ref, k_ref, vt_ref, o_ref, m_ref, l_ref, acc_ref, *,
                 tq, tk, rq, ks, kh, ahead, lambda_init):
    p_idx = pl.program_id(1)
    qi = qi_tab[p_idx]
    ki = ki_tab[p_idx]

    @pl.when(ki == 0)
    def _():
        m_ref[...] = jnp.full(m_ref.shape, -jnp.inf, _F32)
        l_ref[...] = jnp.zeros(l_ref.shape, _F32)
        acc_ref[...] = jnp.zeros(acc_ref.shape, _F32)

    def step(off):

        def span_off(b):
            return None if off is None else off + b * ks

        def keys_used(r, b):
            if off is None:
                return ks
            last_q = (r + 1) * rq - 1
            return kh * sum(1 for h in range(ks // kh) if span_off(b) + h * kh <= last_q)

        def needs_mask(r, b):
            return off is not None and span_off(b) + keys_used(r, b) - 1 > r * rq

        chunks = [(c, r, b) for b in range(tk // ks) for r in range(tq // rq) for c in range(2)
                  if keys_used(r, b) > 0]
        is_last_block = off is not None and off + tk >= tq
        last_chunk_of = {r: i for i, (_, r, _) in enumerate(chunks)}

        def scores(c, r, b):
            sl = slice(c * _HEAD_DIM, (c + 1) * _HEAD_DIM)
            q = q_ref[r * rq:(r + 1) * rq, sl]
            parts = [lax.dot_general(k_ref[b * ks + h * kh:b * ks + (h + 1) * kh, sl], q,
                                     (((1,), (1,)), ((), ())), preferred_element_type=_F32)
                     for h in range(keys_used(r, b) // kh)]
            return jnp.concatenate(parts, axis=0)

        def softmax_pv(c, r, b, st):
            qs = slice(r * rq, (r + 1) * rq)
            nk = keys_used(r, b)
            if needs_mask(r, b):
                delta = (lax.broadcasted_iota(jnp.int32, (nk, rq), 0)
                         - lax.broadcasted_iota(jnp.int32, (nk, rq), 1))
                st = jnp.where(delta <= r * rq - span_off(b), st, -jnp.inf)
            m_prev = m_ref[c, :, qs]
            m_new = jnp.maximum(m_prev, jnp.max(st, axis=0, keepdims=True))
            alpha = jnp.exp2(m_prev - m_new)
            p = jnp.exp2(st - m_new)
            l_ref[c, :, qs] = alpha * l_ref[c, :, qs] + jnp.sum(p, axis=0, keepdims=True)
            acc_ref[c, :, qs] = (alpha * acc_ref[c, :, qs]
                                 + _dot(vt_ref[:, b * ks:b * ks + nk], p.astype(_BF16)))
            m_ref[c, :, qs] = m_new

        pending = [scores(*ch) for ch in chunks[:ahead]]
        if is_last_block:
            for r in range(tq // rq):
                if r not in last_chunk_of:
                    write_out(r)
        for i, ch in enumerate(chunks):
            if i + ahead < len(chunks):
                pending.append(scores(*chunks[i + ahead]))
            softmax_pv(*ch, pending.pop(0))
            if is_last_block and last_chunk_of[ch[1]] == i:
                write_out(ch[1])

    def write_out(r):
        qs = slice(r * rq, (r + 1) * rq)
        lam = (jnp.exp(jnp.sum(lq1_ref[...] * lk1_ref[...], axis=1, keepdims=True))
               - jnp.exp(jnp.sum(lq2_ref[...] * lk2_ref[...], axis=1, keepdims=True))
               + lambda_init)
        ot = (acc_ref[0, :, qs] / l_ref[0, :, qs]
              - lam * (acc_ref[1, :, qs] / l_ref[1, :, qs]))
        ot = ot * lax.rsqrt(jnp.mean(ot * ot, axis=0, keepdims=True) + _LN_EPS)
        ot = ot * subw_ref[...] * (1.0 - lambda_init)
        o_ref[qs, :] = ot.T.astype(o_ref.dtype)

    off = ki * tk - qi * tq

    @pl.when(off + tk <= 0)
    def _():
        step(None)

    for diag_off in range(0, tq, tk):
        @pl.when(off == diag_off)
        def _(diag_off=diag_off):
            step(diag_off)


def _causal_pairs(n_q, tq, tk):
    qs, ks = [], []
    for qi in range(n_q):
        for ki in range(((qi + 1) * tq - 1) // tk + 1):
            qs.append(qi)
            ks.append(ki)
    return np.asarray(qs, np.int32), np.asarray(ks, np.int32)


def _attention(qk, vt, lq1, lk1, lq2, lk2, subw_col, lambda_init,
               tq=2048, tk=2048, rq=512, ks=1024, kh=512, ahead=2):
    s_len = qk.shape[0]
    q_blocks, k_blocks = _causal_pairs(s_len // tq, tq, tk)
    k_blk0 = _SEG // _V_DIM
    vec = lambda n: pl.BlockSpec((1, n), lambda h, p, qt, kt: (0, 0))
    kern = functools.partial(_attn_kernel, tq=tq, tk=tk, rq=rq, ks=ks, kh=kh, ahead=ahead,
                             lambda_init=lambda_init)
    return pl.pallas_call(
        kern,
        grid_spec=pltpu.PrefetchScalarGridSpec(
            num_scalar_prefetch=2,
            grid=(_HEADS, len(q_blocks)),
            in_specs=[
                vec(_HEAD_DIM), vec(_HEAD_DIM), vec(_HEAD_DIM), vec(_HEAD_DIM),
                pl.BlockSpec((_V_DIM, 1), lambda h, p, qt, kt: (0, 0)),
                pl.BlockSpec((tq, _V_DIM), lambda h, p, qt, kt: (qt[p], h)),
                pl.BlockSpec((tk, _V_DIM), lambda h, p, qt, kt: (kt[p], k_blk0 + h)),
                pl.BlockSpec((_V_DIM, tk), lambda h, p, qt, kt: (h, kt[p])),
            ],
            out_specs=pl.BlockSpec((tq, _V_DIM), lambda h, p, qt, kt: (qt[p], h)),
            scratch_shapes=[
                pltpu.VMEM((2, 1, tq), _F32),
                pltpu.VMEM((2, 1, tq), _F32),
                pltpu.VMEM((2, _V_DIM, tq), _F32),
            ],
        ),
        out_shape=jax.ShapeDtypeStruct((s_len, _HEADS * _V_DIM), _BF16),
        compiler_params=_params(("arbitrary", "arbitrary")),
        name="diff_attn",
    )(jnp.asarray(q_blocks), jnp.asarray(k_blocks), lq1, lk1, lq2, lk2, subw_col, qk, qk, vt)


def _sgu_kernel(u_ref, s_ref, g_ref, b_ref, w_ref, bs_ref, o_ref, *, tm):
    gd = _SEG // _SGU_GROUPS
    sn = _layer_norm_rows(s_ref[...].astype(_F32), g_ref[...], b_ref[...]).astype(_BF16)
    row = lax.broadcasted_iota(jnp.int32, (_CHUNK, _CHUNK), 0)
    col = lax.broadcasted_iota(jnp.int32, (_CHUNK, _CHUNK), 1)
    for g in range(_SGU_GROUPS):
        w = jnp.where(row >= col, w_ref[g], 0.0).astype(_BF16)
        bias = bs_ref[:, g:g + 1]
        cs = slice(g * gd, (g + 1) * gd)
        for c in range(tm // _CHUNK):
            rs = slice(c * _CHUNK, (c + 1) * _CHUNK)
            mixed = _dot(w, sn[rs, cs]) + bias
            o_ref[rs, cs] = (u_ref[rs, cs].astype(_F32) * mixed).astype(o_ref.dtype)


def _sgu(us, ln_g, ln_b, w_s, b_s_t, tm=512):
    s_len = us.shape[0]
    return pl.pallas_call(
        functools.partial(_sgu_kernel, tm=tm),
        grid=(s_len // tm,),
        in_specs=[
            pl.BlockSpec((tm, _SEG), lambda i: (i, 0)),
            pl.BlockSpec((tm, _SEG), lambda i: (i, 1)),
            pl.BlockSpec((1, _SEG), lambda i: (0, 0)),
            pl.BlockSpec((1, _SEG), lambda i: (0, 0)),
            pl.BlockSpec((_SGU_GROUPS, _CHUNK, _CHUNK), lambda i: (0, 0, 0)),
            pl.BlockSpec((_CHUNK, _SGU_GROUPS), lambda i: (0, 0)),
        ],
        out_specs=pl.BlockSpec((tm, _SEG), lambda i: (i, 0)),
        out_shape=jax.ShapeDtypeStruct((s_len, _SEG), _BF16),
        compiler_params=_params(("arbitrary",)),
        name="sgu",
    )(us, us, ln_g, ln_b, w_s, b_s_t)


def _merge_kernel(ya_ref, yb_ref, wa_ref, wb_ref, ga_ref, gb_ref, o_ref):
    wa = wa_ref[...].astype(_BF16)
    wb = wb_ref[...].astype(_BF16)
    slab = ya_ref.shape[0] // _ROW_SPLITS
    for s in range(_ROW_SPLITS):
        rows = slice(s * slab, (s + 1) * slab)
        a = _dot(ya_ref[rows, :], wa)
        b = _dot(yb_ref[rows, :], wb)
        o_ref[rows, :] = (ga_ref[rows, :].astype(_F32) * a
                          + gb_ref[rows, :].astype(_F32) * b).astype(o_ref.dtype)


def _merge(y_a, y_b, w_a, w_b, gates, tm=_TM, tn=_TN):
    m, k = y_a.shape
    n = w_a.shape[1]
    return pl.pallas_call(
        _merge_kernel,
        grid=(m // tm, n // tn),
        in_specs=[
            pl.BlockSpec((tm, k), lambda i, j: (i, 0)),
            pl.BlockSpec((tm, k), lambda i, j: (i, 0)),
            pl.BlockSpec((k, tn), lambda i, j: (0, j)),
            pl.BlockSpec((k, tn), lambda i, j: (0, j)),
            _tile_spec(tm, tn),
            _tile_spec(tm, tn, n // tn),
        ],
        out_specs=pl.BlockSpec((tm, tn), lambda i, j: (i, j)),
        out_shape=jax.ShapeDtypeStruct((m, n), _BF16),
        compiler_params=_params(("arbitrary", "arbitrary")),
        name="merge",
    )(y_a, y_b, w_a, w_b, gates, gates)


def _ln_bf16_kernel(x_ref, g_ref, b_ref, o_ref):
    o_ref[...] = _layer_norm_rows(x_ref[...], g_ref[...], b_ref[...]).astype(o_ref.dtype)


def _ln_bf16(x, g, b, tm=256):
    m, n = x.shape
    row = pl.BlockSpec((tm, n), lambda i: (i, 0))
    vec = pl.BlockSpec((1, n), lambda i: (0, 0))
    return pl.pallas_call(
        _ln_bf16_kernel,
        grid=(m // tm,),
        in_specs=[row, vec, vec],
        out_specs=row,
        out_shape=jax.ShapeDtypeStruct((m, n), _BF16),
        compiler_params=_params(("arbitrary",)),
        name="ln1",
    )(x, g, b)


def _final_kernel(pre_ref, ff_ref, g1_ref, b1_ref, bo_ref, g2_ref, b2_ref, o_ref):
    x = pre_ref[...]
    xc = x - jnp.mean(x, axis=-1, keepdims=True)
    var = jnp.mean(xc * xc, axis=-1, keepdims=True)
    shift = _ALPHA * b1_ref[...] + bo_ref[...]
    y = xc * (_ALPHA * lax.rsqrt(var + _LN_EPS)) * g1_ref[...] + (ff_ref[...] + shift)
    o_ref[...] = _layer_norm_rows(y, g2_ref[...], b2_ref[...])


def _final(pre1, ff, g1, b1, b_out, g2, b2, tm=256):
    m, n = pre1.shape
    row = pl.BlockSpec((tm, n), lambda i: (i, 0))
    vec = pl.BlockSpec((1, n), lambda i: (0, 0))
    return pl.pallas_call(
        _final_kernel,
        grid=(m // tm,),
        in_specs=[row, row, vec, vec, vec, vec, vec],
        out_specs=row,
        out_shape=jax.ShapeDtypeStruct((m, n), _F32),
        compiler_params=_params(("arbitrary",)),
        name="final_ln",
    )(pre1, ff, g1, b1, b_out, g2, b2)


def _mlp_out_kernel(z_ref, w_ref, o_ref):
    kk = pl.program_id(2)
    slab = z_ref.shape[0] // _ROW_SPLITS

    def accumulate(first):
        wb = w_ref[...].astype(_BF16)
        for s in range(_ROW_SPLITS):
            rows = slice(s * slab, (s + 1) * slab)
            part = _dot(z_ref[rows, :], wb)
            o_ref[rows, :] = part if first else o_ref[rows, :] + part

    @pl.when(kk == 0)
    def _():
        accumulate(True)

    @pl.when(kk > 0)
    def _():
        accumulate(False)


def _mlp_out(z, w, tm=2048, tn=1024, tk=1024):
    m, k = z.shape
    n = w.shape[1]
    return pl.pallas_call(
        _mlp_out_kernel,
        grid=(m // tm, n // tn, k // tk),
        in_specs=[
            pl.BlockSpec((tm, tk), lambda i, j, kk: (i, kk)),
            pl.BlockSpec((tk, tn), lambda i, j, kk: (kk, j)),
        ],
        out_specs=pl.BlockSpec((tm, tn), lambda i, j, kk: (i, j)),
        out_shape=jax.ShapeDtypeStruct((m, n), _F32),
        compiler_params=_params(("arbitrary", "arbitrary", "arbitrary")),
        name="mlp_out",
    )(z, w)


def _rope_tables(s_len):
    inv_freq = _ROPE_THETA ** (-np.arange(0, _ROPE_DIM, 2, dtype=np.float64) / _ROPE_DIM)
    ang = np.arange(s_len, dtype=np.float64)[:, None] * inv_freq[None, :]
    cos, sin = np.cos(ang), np.sin(ang)
    ones = np.ones((s_len, _LANES - _ROPE_DIM))
    zeros_h = np.zeros((s_len, _ROPE_HALF))
    zeros_r = np.zeros((s_len, _LANES - _ROPE_DIM))
    cos_t = np.concatenate([cos, cos, ones], axis=1)
    sin_up = np.concatenate([zeros_h, sin, zeros_r], axis=1)
    sin_dn = np.concatenate([-sin, zeros_h, zeros_r], axis=1)
    return tuple(jnp.asarray(t, dtype=_F32) for t in (cos_t, sin_up, sin_dn))


def kernel(x, w_in, lambda_q1, lambda_k1, lambda_q2, lambda_k2, subln_w, sgu_ln_g, sgu_ln_b,
           w_spatial, b_spatial, w_proj_attn, w_proj_sgu, w_out, ln1_g, ln1_b, w_mlp_in, b_mlp_in,
           w_mlp_out, b_mlp_out, ln2_g, ln2_b):
    bsz, s_len, d = x.shape
    assert bsz == 1 and w_in.shape[0] == _DEPTH == 1
    h = x.reshape(s_len, d)
    l = 0
    lambda_init = 0.8 - 0.6 * math.exp(-0.3 * l)
    cos_t, sin_up, sin_dn = _rope_tables(s_len)
    q_scale = _HEAD_DIM ** -0.5 * math.log2(math.e)
    qk_scale = jnp.concatenate([jnp.full((1, _SEG), q_scale, _F32), jnp.ones((1, _SEG), _F32)], axis=1)

    hb = _cast_bf16(h)
    w_in_l = w_in[l]
    qk = _linear("proj_qk", hb, w_in_l, 0, 2 * _SEG, _rope_epilogue,
                 extras=[(cos_t, _row_spec(_TM, _LANES)), (sin_up, _row_spec(_TM, _LANES)),
                         (sin_dn, _row_spec(_TM, _LANES)), (qk_scale, _col_spec(_TN))],
                 row_splits=_ROW_SPLITS)
    vt = _linear("proj_v", hb, w_in_l, 2 * _SEG, _SEG, _transpose_epilogue, transposed_out=True)
    us = _linear("proj_us", hb, w_in_l, 3 * _SEG, 2 * _SEG, _gelu_epilogue)
    gates = _linear("proj_gates", hb, w_in_l, 5 * _SEG, 2 * d, _sigmoid_epilogue,
                    row_splits=_ROW_SPLITS)

    y_a = _attention(qk, vt, lambda_q1[l][None], lambda_k1[l][None], lambda_q2[l][None],
                     lambda_k2[l][None], subln_w[l][:, None], lambda_init)
    y_b = _sgu(us, sgu_ln_g[l][None], sgu_ln_b[l][None], w_spatial[l], b_spatial[l].T)
    merged = _merge(y_a, y_b, w_proj_attn[l], w_proj_sgu[l], gates)
    pre1 = _linear("outproj", merged, w_out[l], 0, d, _residual_epilogue,
                   extras=[(h, _tile_spec(_TM, _TN))], out_dtype=_F32)
    h1b = _ln_bf16(pre1, ln1_g[l][None], ln1_b[l][None])
    z = _linear("mlp_in", h1b, w_mlp_in[l], 0, w_mlp_in.shape[2], _relu2_epilogue,
                extras=[(b_mlp_in[l][None], _col_spec(_TN))])
    ff = _mlp_out(z, w_mlp_out[l])
    out = _final(pre1, ff, ln1_g[l][None], ln1_b[l][None], b_mlp_out[l][None],
                 ln2_g[l][None], ln2_b[l][None])
    return out.reshape(bsz, s_len, d)
```

```python
import functools
import math

import numpy as np
import jax
import jax.numpy as jnp
from jax import lax
from jax.experimental import pallas as pl
from jax.experimental.pallas import tpu as pltpu

_F32 = jnp.float32
_BF16 = jnp.bfloat16

_LANES = 128
_VMEM_LIMIT_BYTES = 60 * 1024 * 1024

_HEADS = 8
_HEAD_DIM = 128
_V_DIM = 2 * _HEAD_DIM
_ROPE_DIM = _HEAD_DIM // 4
_ROPE_HALF = _ROPE_DIM // 2
_ROPE_THETA = 500000.0
_SGU_GROUPS = 8
_CHUNK = 128
_DEPTH = 1
_ALPHA = (2.0 * _DEPTH) ** 0.25
_LN_EPS = 1e-5
_SEG = 2048

_TM = 1024
_TN = 512
_TN_BF16 = 1024
_ROW_SPLITS = 4
_CAST_COL_BLOCKS = 8


def _params(sem):
    return pltpu.CompilerParams(dimension_semantics=sem, vmem_limit_bytes=_VMEM_LIMIT_BYTES)


def _dot(a, b):
    return jnp.dot(a, b, preferred_element_type=_F32)


def _layer_norm_rows(x, g, b):
    mu = jnp.mean(x, axis=-1, keepdims=True)
    xc = x - mu
    var = jnp.mean(xc * xc, axis=-1, keepdims=True)
    return xc * lax.rsqrt(var + _LN_EPS) * g + b


def _cast_kernel(x_ref, o_ref):
    o_ref[...] = x_ref[...].astype(o_ref.dtype)


def _cast_bf16(x, tm=512):
    m, n = x.shape
    return pl.pallas_call(
        _cast_kernel,
        grid=(m // tm,),
        in_specs=[pl.BlockSpec((tm, n), lambda i: (i, 0))],
        out_specs=pl.BlockSpec((tm, n), lambda i: (i, 0)),
        out_shape=jax.ShapeDtypeStruct((m, n), _BF16),
        compiler_params=_params(("arbitrary",)),
        name="cast_x",
    )(x)


def _linear_kernel(a_ref, w_ref, *rest, epilogue, row_splits):
    *extras, o_ref = rest
    wb = w_ref[...].astype(_BF16)
    slab = a_ref.shape[0] // row_splits
    for s in range(row_splits):
        rows = slice(s * slab, (s + 1) * slab)
        epilogue(_dot(a_ref[rows, :], wb), rows, o_ref, *extras)


def _row_spec(tm, width):
    return pl.BlockSpec((tm, width), lambda i, j: (i, 0))


def _col_spec(tn, blk0=0):
    return pl.BlockSpec((1, tn), lambda i, j: (0, blk0 + j))


def _tile_spec(tm, tn, blk0=0):
    return pl.BlockSpec((tm, tn), lambda i, j: (i, blk0 + j))


def _linear(name, a, w, col0, n_cols, epilogue, extras=(), out_dtype=_BF16, tm=_TM, tn=_TN,
            transposed_out=False, row_splits=1):
    m, k = a.shape
    blk0 = col0 // tn
    arrays = [x for x, _ in extras]
    specs = [s for _, s in extras]
    if transposed_out:
        out_spec = pl.BlockSpec((tn, tm), lambda i, j: (j, i))
        out_shape = jax.ShapeDtypeStruct((n_cols, m), out_dtype)
    else:
        out_spec = pl.BlockSpec((tm, tn), lambda i, j: (i, j))
        out_shape = jax.ShapeDtypeStruct((m, n_cols), out_dtype)
    return pl.pallas_call(
        functools.partial(_linear_kernel, epilogue=epilogue, row_splits=row_splits),
        grid=(m // tm, n_cols // tn),
        in_specs=[
            pl.BlockSpec((tm, k), lambda i, j: (i, 0)),
            pl.BlockSpec((k, tn), lambda i, j: (0, blk0 + j)),
            *specs,
        ],
        out_specs=out_spec,
        out_shape=out_shape,
        compiler_params=_params(("arbitrary", "arbitrary")),
        name=name,
    )(a, w, *arrays)


def _transpose_epilogue(acc, rows, o_ref):
    o_ref[:, rows] = acc.T.astype(o_ref.dtype)


def _gelu_epilogue(acc, rows, o_ref):
    o_ref[rows, :] = jax.nn.gelu(acc).astype(o_ref.dtype)


def _sigmoid_epilogue(acc, rows, o_ref):
    o_ref[rows, :] = jax.nn.sigmoid(acc).astype(o_ref.dtype)


def _rope_epilogue(acc, rows, o_ref, cos_ref, sup_ref, sdn_ref, scale_ref):
    cos, sup, sdn = cos_ref[rows, :], sup_ref[rows, :], sdn_ref[rows, :]
    for c in range(acc.shape[1] // _LANES):
        sl = slice(c * _LANES, (c + 1) * _LANES)
        t = acc[:, sl]
        r = (t * cos
             + pltpu.roll(t, _ROPE_HALF, 1) * sup
             + pltpu.roll(t, _LANES - _ROPE_HALF, 1) * sdn)
        o_ref[rows, sl] = (r * scale_ref[:, sl]).astype(o_ref.dtype)


def _residual_epilogue(acc, rows, o_ref, x_ref):
    o_ref[rows, :] = _ALPHA * x_ref[rows, :] + acc


def _relu2_epilogue(acc, rows, o_ref, b_ref):
    z = jnp.maximum(acc + b_ref[...], 0.0)
    o_ref[rows, :] = (z * z).astype(o_ref.dtype)


def _attn_kernel(qi_tab, ki_tab, lq1_ref, lk1_ref, lq2_ref, lk2_ref, subw_ref,
                 q_ref, k_ref, vt_ref, *rest, n_cast, tq, tk, rq, ks, kh, ahead, lambda_init):
    cast_in = rest[:n_cast]
    o_ref = rest[n_cast]
    cast_out = rest[n_cast + 1:2 * n_cast + 1]
    m_ref, l_ref, acc_ref = rest[2 * n_cast + 1:]
    p_idx = pl.program_id(1)
    qi = qi_tab[p_idx]
    ki = ki_tab[p_idx]

    @pl.when(ki == 0)
    def _():
        m_ref[...] = jnp.full(m_ref.shape, -jnp.inf, _F32)
        l_ref[...] = jnp.zeros(l_ref.shape, _F32)
        acc_ref[...] = jnp.zeros(acc_ref.shape, _F32)

    def step(off):

        def span_off(b):
            return None if off is None else off + b * ks

        def keys_used(r, b):
            if off is None:
                return ks
            last_q = (r + 1) * rq - 1
            return kh * sum(1 for h in range(ks // kh) if span_off(b) + h * kh <= last_q)

        def needs_mask(r, b):
            return off is not None and span_off(b) + keys_used(r, b) - 1 > r * rq

        chunks = [(c, r, b) for b in range(tk // ks) for r in range(tq // rq) for c in range(2)
                  if keys_used(r, b) > 0]
        is_last_block = off is not None and off + tk >= tq
        last_chunk_of = {r: i for i, (_, r, _) in enumerate(chunks)}

        def scores(c, r, b):
            sl = slice(c * _HEAD_DIM, (c + 1) * _HEAD_DIM)
            q = q_ref[r * rq:(r + 1) * rq, sl]
            parts = [lax.dot_general(k_ref[b * ks + h * kh:b * ks + (h + 1) * kh, sl], q,
                                     (((1,), (1,)), ((), ())), preferred_element_type=_F32)
                     for h in range(keys_used(r, b) // kh)]
            return jnp.concatenate(parts, axis=0)

        def softmax_pv(c, r, b, st):
            qs = slice(r * rq, (r + 1) * rq)
            nk = keys_used(r, b)
            if needs_mask(r, b):
                delta = (lax.broadcasted_iota(jnp.int32, (nk, rq), 0)
                         - lax.broadcasted_iota(jnp.int32, (nk, rq), 1))
                st = jnp.where(delta <= r * rq - span_off(b), st, -jnp.inf)
            m_prev = m_ref[c, :, qs]
            m_new = jnp.maximum(m_prev, jnp.max(st, axis=0, keepdims=True))
            alpha = jnp.exp2(m_prev - m_new)
            p = jnp.exp2(st - m_new)
            l_ref[c, :, qs] = alpha * l_ref[c, :, qs] + jnp.sum(p, axis=0, keepdims=True)
            acc_ref[c, :, qs] = (alpha * acc_ref[c, :, qs]
                                 + _dot(vt_ref[:, b * ks:b * ks + nk], p.astype(_BF16)))
            m_ref[c, :, qs] = m_new

        pending = [scores(*ch) for ch in chunks[:ahead]]
        for src, dst in zip(cast_in, cast_out):
            dst[...] = src[...].astype(dst.dtype)
        if is_last_block:
            for r in range(tq // rq):
                if r not in last_chunk_of:
                    write_out(r)
        for i, ch in enumerate(chunks):
            if i + ahead < len(chunks):
                pending.append(scores(*chunks[i + ahead]))
            softmax_pv(*ch, pending.pop(0))
            if is_last_block and last_chunk_of[ch[1]] == i:
                write_out(ch[1])

    def write_out(r):
        qs = slice(r * rq, (r + 1) * rq)
        lam = (jnp.exp(jnp.sum(lq1_ref[...] * lk1_ref[...], axis=1, keepdims=True))
               - jnp.exp(jnp.sum(lq2_ref[...] * lk2_ref[...], axis=1, keepdims=True))
               + lambda_init)
        ot = (acc_ref[0, :, qs] / l_ref[0, :, qs]
              - lam * (acc_ref[1, :, qs] / l_ref[1, :, qs]))
        ot = ot * lax.rsqrt(jnp.mean(ot * ot, axis=0, keepdims=True) + _LN_EPS)
        ot = ot * subw_ref[...] * (1.0 - lambda_init)
        o_ref[qs, :] = ot.T.astype(o_ref.dtype)

    off = ki * tk - qi * tq

    @pl.when(off + tk <= 0)
    def _():
        step(None)

    for diag_off in range(0, tq, tk):
        @pl.when(off == diag_off)
        def _(diag_off=diag_off):
            step(diag_off)


def _causal_pairs(n_q, tq, tk):
    qs, ks = [], []
    for qi in range(n_q):
        for ki in range(((qi + 1) * tq - 1) // tk + 1):
            qs.append(qi)
            ks.append(ki)
    return np.asarray(qs, np.int32), np.asarray(ks, np.int32)


def _attention(qk, vt, lq1, lk1, lq2, lk2, subw_col, lambda_init, cast_weights,
               tq=2048, tk=2048, rq=512, ks=1024, kh=512, ahead=2):
    s_len = qk.shape[0]
    q_blocks, k_blocks = _causal_pairs(s_len // tq, tq, tk)
    n_pairs = len(q_blocks)
    assert n_pairs >= _CAST_COL_BLOCKS
    k_blk0 = _SEG // _V_DIM
    vec = lambda n: pl.BlockSpec((1, n), lambda h, p, qt, kt: (0, 0))

    def cast_spec(w):
        r, c = w.shape
        return pl.BlockSpec((r // _HEADS, c // _CAST_COL_BLOCKS),
                            lambda h, p, qt, kt: (h, jnp.minimum(p, _CAST_COL_BLOCKS - 1)))

    cast_specs = [cast_spec(w) for w in cast_weights]
    kern = functools.partial(_attn_kernel, n_cast=len(cast_weights), tq=tq, tk=tk, rq=rq, ks=ks,
                             kh=kh, ahead=ahead, lambda_init=lambda_init)
    outs = pl.pallas_call(
        kern,
        grid_spec=pltpu.PrefetchScalarGridSpec(
            num_scalar_prefetch=2,
            grid=(_HEADS, n_pairs),
            in_specs=[
                vec(_HEAD_DIM), vec(_HEAD_DIM), vec(_HEAD_DIM), vec(_HEAD_DIM),
                pl.BlockSpec((_V_DIM, 1), lambda h, p, qt, kt: (0, 0)),
                pl.BlockSpec((tq, _V_DIM), lambda h, p, qt, kt: (qt[p], h)),
                pl.BlockSpec((tk, _V_DIM), lambda h, p, qt, kt: (kt[p], k_blk0 + h)),
                pl.BlockSpec((_V_DIM, tk), lambda h, p, qt, kt: (h, kt[p])),
                *cast_specs,
            ],
            out_specs=[pl.BlockSpec((tq, _V_DIM), lambda h, p, qt, kt: (qt[p], h)), *cast_specs],
            scratch_shapes=[
                pltpu.VMEM((2, 1, tq), _F32),
                pltpu.VMEM((2, 1, tq), _F32),
                pltpu.VMEM((2, _V_DIM, tq), _F32),
            ],
        ),
        out_shape=[jax.ShapeDtypeStruct((s_len, _HEADS * _V_DIM), _BF16),
                   *[jax.ShapeDtypeStruct(w.shape, _BF16) for w in cast_weights]],
        compiler_params=_params(("arbitrary", "arbitrary")),
        name="diff_attn",
    )(jnp.asarray(q_blocks), jnp.asarray(k_blocks), lq1, lk1, lq2, lk2, subw_col, qk, qk, vt,
      *cast_weights)
    return outs[0], outs[1:]


def _sgu_kernel(u_ref, s_ref, g_ref, b_ref, w_ref, bs_ref, o_ref, *, tm):
    gd = _SEG // _SGU_GROUPS
    sn = _layer_norm_rows(s_ref[...].astype(_F32), g_ref[...], b_ref[...]).astype(_BF16)
    row = lax.broadcasted_iota(jnp.int32, (_CHUNK, _CHUNK), 0)
    col = lax.broadcasted_iota(jnp.int32, (_CHUNK, _CHUNK), 1)
    for g in range(_SGU_GROUPS):
        w = jnp.where(row >= col, w_ref[g], 0.0).astype(_BF16)
        bias = bs_ref[:, g:g + 1]
        cs = slice(g * gd, (g + 1) * gd)
        for c in range(tm // _CHUNK):
            rs = slice(c * _CHUNK, (c + 1) * _CHUNK)
            mixed = _dot(w, sn[rs, cs]) + bias
            o_ref[rs, cs] = (u_ref[rs, cs].astype(_F32) * mixed).astype(o_ref.dtype)


def _sgu(us, ln_g, ln_b, w_s, b_s_t, tm=512):
    s_len = us.shape[0]
    return pl.pallas_call(
        functools.partial(_sgu_kernel, tm=tm),
        grid=(s_len // tm,),
        in_specs=[
            pl.BlockSpec((tm, _SEG), lambda i: (i, 0)),
            pl.BlockSpec((tm, _SEG), lambda i: (i, 1)),
            pl.BlockSpec((1, _SEG), lambda i: (0, 0)),
            pl.BlockSpec((1, _SEG), lambda i: (0, 0)),
            pl.BlockSpec((_SGU_GROUPS, _CHUNK, _CHUNK), lambda i: (0, 0, 0)),
            pl.BlockSpec((_CHUNK, _SGU_GROUPS), lambda i: (0, 0)),
        ],
        out_specs=pl.BlockSpec((tm, _SEG), lambda i: (i, 0)),
        out_shape=jax.ShapeDtypeStruct((s_len, _SEG), _BF16),
        compiler_params=_params(("arbitrary",)),
        name="sgu",
    )(us, us, ln_g, ln_b, w_s, b_s_t)


def _merge_kernel(ya_ref, yb_ref, wa_ref, wb_ref, ga_ref, gb_ref, o_ref):
    wa = wa_ref[...].astype(_BF16)
    wb = wb_ref[...].astype(_BF16)
    slab = ya_ref.shape[0] // _ROW_SPLITS
    for s in range(_ROW_SPLITS):
        rows = slice(s * slab, (s + 1) * slab)
        a = _dot(ya_ref[rows, :], wa)
        b = _dot(yb_ref[rows, :], wb)
        o_ref[rows, :] = (ga_ref[rows, :].astype(_F32) * a
                          + gb_ref[rows, :].astype(_F32) * b).astype(o_ref.dtype)


def _merge(y_a, y_b, w_a, w_b, gates, tm=_TM, tn=_TN_BF16):
    m, k = y_a.shape
    n = w_a.shape[1]
    return pl.pallas_call(
        _merge_kernel,
        grid=(m // tm, n // tn),
        in_specs=[
            pl.BlockSpec((tm, k), lambda i, j: (i, 0)),
            pl.BlockSpec((tm, k), lambda i, j: (i, 0)),
            pl.BlockSpec((k, tn), lambda i, j: (0, j)),
            pl.BlockSpec((k, tn), lambda i, j: (0, j)),
            _tile_spec(tm, tn),
            _tile_spec(tm, tn, n // tn),
        ],
        out_specs=pl.BlockSpec((tm, tn), lambda i, j: (i, j)),
        out_shape=jax.ShapeDtypeStruct((m, n), _BF16),
        compiler_params=_params(("arbitrary", "arbitrary")),
        name="merge",
    )(y_a, y_b, w_a, w_b, gates, gates)


def _ln_bf16_kernel(x_ref, g_ref, b_ref, o_ref):
    o_ref[...] = _layer_norm_rows(x_ref[...], g_ref[...], b_ref[...]).astype(o_ref.dtype)


def _ln_bf16(x, g, b, tm=256):
    m, n = x.shape
    row = pl.BlockSpec((tm, n), lambda i: (i, 0))
    vec = pl.BlockSpec((1, n), lambda i: (0, 0))
    return pl.pallas_call(
        _ln_bf16_kernel,
        grid=(m // tm,),
        in_specs=[row, vec, vec],
        out_specs=row,
        out_shape=jax.ShapeDtypeStruct((m, n), _BF16),
        compiler_params=_params(("arbitrary",)),
        name="ln1",
    )(x, g, b)


def _final_kernel(pre_ref, ff_ref, g1_ref, b1_ref, bo_ref, g2_ref, b2_ref, o_ref):
    x = pre_ref[...]
    xc = x - jnp.mean(x, axis=-1, keepdims=True)
    var = jnp.mean(xc * xc, axis=-1, keepdims=True)
    shift = _ALPHA * b1_ref[...] + bo_ref[...]
    y = xc * (_ALPHA * lax.rsqrt(var + _LN_EPS)) * g1_ref[...] + (ff_ref[...] + shift)
    o_ref[...] = _layer_norm_rows(y, g2_ref[...], b2_ref[...])


def _final(pre1, ff, g1, b1, b_out, g2, b2, tm=256):
    m, n = pre1.shape
    row = pl.BlockSpec((tm, n), lambda i: (i, 0))
    vec = pl.BlockSpec((1, n), lambda i: (0, 0))
    return pl.pallas_call(
        _final_kernel,
        grid=(m // tm,),
        in_specs=[row, row, vec, vec, vec, vec, vec],
        out_specs=row,
        out_shape=jax.ShapeDtypeStruct((m, n), _F32),
        compiler_params=_params(("arbitrary",)),
        name="final_ln",
    )(pre1, ff, g1, b1, b_out, g2, b2)


def _mlp_out_kernel(z_ref, w_ref, o_ref):
    kk = pl.program_id(2)
    slab = z_ref.shape[0] // _ROW_SPLITS

    def accumulate(first):
        wb = w_ref[...].astype(_BF16)
        for s in range(_ROW_SPLITS):
            rows = slice(s * slab, (s + 1) * slab)
            part = _dot(z_ref[rows, :], wb)
            o_ref[rows, :] = part if first else o_ref[rows, :] + part

    @pl.when(kk == 0)
    def _():
        accumulate(True)

    @pl.when(kk > 0)
    def _():
        accumulate(False)


def _mlp_out(z, w, tm=2048, tn=1024, tk=2048):
    m, k = z.shape
    n = w.shape[1]
    return pl.pallas_call(
        _mlp_out_kernel,
        grid=(m // tm, n // tn, k // tk),
        in_specs=[
            pl.BlockSpec((tm, tk), lambda i, j, kk: (i, kk)),
            pl.BlockSpec((tk, tn), lambda i, j, kk: (kk, j)),
        ],
        out_specs=pl.BlockSpec((tm, tn), lambda i, j, kk: (i, j)),
        out_shape=jax.ShapeDtypeStruct((m, n), _F32),
        compiler_params=_params(("arbitrary", "arbitrary", "arbitrary")),
        name="mlp_out",
    )(z, w)


def _rope_tables(s_len):
    inv_freq = _ROPE_THETA ** (-np.arange(0, _ROPE_DIM, 2, dtype=np.float64) / _ROPE_DIM)
    ang = np.arange(s_len, dtype=np.float64)[:, None] * inv_freq[None, :]
    cos, sin = np.cos(ang), np.sin(ang)
    ones = np.ones((s_len, _LANES - _ROPE_DIM))
    zeros_h = np.zeros((s_len, _ROPE_HALF))
    zeros_r = np.zeros((s_len, _LANES - _ROPE_DIM))
    cos_t = np.concatenate([cos, cos, ones], axis=1)
    sin_up = np.concatenate([zeros_h, sin, zeros_r], axis=1)
    sin_dn = np.concatenate([-sin, zeros_h, zeros_r], axis=1)
    return tuple(jnp.asarray(t, dtype=_F32) for t in (cos_t, sin_up, sin_dn))


def kernel(x, w_in, lambda_q1, lambda_k1, lambda_q2, lambda_k2, subln_w, sgu_ln_g, sgu_ln_b,
           w_spatial, b_spatial, w_proj_attn, w_proj_sgu, w_out, ln1_g, ln1_b, w_mlp_in, b_mlp_in,
           w_mlp_out, b_mlp_out, ln2_g, ln2_b):
    bsz, s_len, d = x.shape
    assert bsz == 1 and w_in.shape[0] == _DEPTH == 1
    h = x.reshape(s_len, d)
    l = 0
    lambda_init = 0.8 - 0.6 * math.exp(-0.3 * l)
    cos_t, sin_up, sin_dn = _rope_tables(s_len)
    q_scale = _HEAD_DIM ** -0.5 * math.log2(math.e)
    qk_scale = jnp.concatenate([jnp.full((1, _SEG), q_scale, _F32), jnp.ones((1, _SEG), _F32)], axis=1)

    hb = _cast_bf16(h)
    w_in_l = w_in[l]
    qk = _linear("proj_qk", hb, w_in_l, 0, 2 * _SEG, _rope_epilogue,
                 extras=[(cos_t, _row_spec(_TM, _LANES)), (sin_up, _row_spec(_TM, _LANES)),
                         (sin_dn, _row_spec(_TM, _LANES)), (qk_scale, _col_spec(_TN))],
                 row_splits=_ROW_SPLITS)
    vt = _linear("proj_v", hb, w_in_l, 2 * _SEG, _SEG, _transpose_epilogue, transposed_out=True)
    us = _linear("proj_us", hb, w_in_l, 3 * _SEG, 2 * _SEG, _gelu_epilogue)
    gates = _linear("proj_gates", hb, w_in_l, 5 * _SEG, 2 * d, _sigmoid_epilogue,
                    row_splits=_ROW_SPLITS)

    y_a, (w_pa, w_ps, w_o, w_mi, w_mo) = _attention(
        qk, vt, lambda_q1[l][None], lambda_k1[l][None], lambda_q2[l][None], lambda_k2[l][None],
        subln_w[l][:, None], lambda_init,
        cast_weights=[w_proj_attn[l], w_proj_sgu[l], w_out[l], w_mlp_in[l], w_mlp_out[l]])
    y_b = _sgu(us, sgu_ln_g[l][None], sgu_ln_b[l][None], w_spatial[l], b_spatial[l].T)
    merged = _merge(y_a, y_b, w_pa, w_ps, gates)
    pre1 = _linear("outproj", merged, w_o, 0, d, _residual_epilogue,
                   extras=[(h, _tile_spec(_TM, _TN_BF16))], out_dtype=_F32, tn=_TN_BF16)
    h1b = _ln_bf16(pre1, ln1_g[l][None], ln1_b[l][None])
    z = _linear("mlp_in", h1b, w_mi, 0, w_mi.shape[1], _relu2_epilogue,
                extras=[(b_mlp_in[l][None], _col_spec(_TN_BF16))], tn=_TN_BF16)
    ff = _mlp_out(z, w_mo)
    out = _final(pre1, ff, ln1_g[l][None], ln1_b[l][None], b_mlp_out[l][None],
                 ln2_g[l][None], ln2_b[l][None])
    return out.reshape(bsz, s_len, d)
```

```python
import functools
import math

import numpy as np
import jax
import jax.numpy as jnp
from jax import lax
from jax.experimental import pallas as pl
from jax.experimental.pallas import tpu as pltpu

_F32 = jnp.float32
_BF16 = jnp.bfloat16

_LANES = 128
_VMEM_LIMIT_BYTES = 60 * 1024 * 1024

_HEADS = 8
_HEAD_DIM = 128
_V_DIM = 2 * _HEAD_DIM
_ROPE_DIM = _HEAD_DIM // 4
_ROPE_HALF = _ROPE_DIM // 2
_ROPE_THETA = 500000.0
_SGU_GROUPS = 8
_CHUNK = 128
_DEPTH = 1
_ALPHA = (2.0 * _DEPTH) ** 0.25
_LN_EPS = 1e-5
_SEG = 2048

_TM = 1024
_TN = 512
_TN_BF16 = 1024
_ROW_SPLITS = 4
_CAST_COL_BLOCKS = 8
_CAST_PIECES = 8


def _params(sem):
    return pltpu.CompilerParams(dimension_semantics=sem, vmem_limit_bytes=_VMEM_LIMIT_BYTES)


def _dot(a, b):
    return jnp.dot(a, b, preferred_element_type=_F32)


def _layer_norm_rows(x, g, b):
    mu = jnp.mean(x, axis=-1, keepdims=True)
    xc = x - mu
    var = jnp.mean(xc * xc, axis=-1, keepdims=True)
    return xc * lax.rsqrt(var + _LN_EPS) * g + b


def _cast_kernel(x_ref, o_ref):
    o_ref[...] = x_ref[...].astype(o_ref.dtype)


def _cast_bf16(x, tm=512):
    m, n = x.shape
    return pl.pallas_call(
        _cast_kernel,
        grid=(m // tm,),
        in_specs=[pl.BlockSpec((tm, n), lambda i: (i, 0))],
        out_specs=pl.BlockSpec((tm, n), lambda i: (i, 0)),
        out_shape=jax.ShapeDtypeStruct((m, n), _BF16),
        compiler_params=_params(("arbitrary",)),
        name="cast_x",
    )(x)


def _linear_kernel(a_ref, w_ref, *rest, epilogue, row_splits):
    *extras, o_ref = rest
    wb = w_ref[...].astype(_BF16)
    slab = a_ref.shape[0] // row_splits
    for s in range(row_splits):
        rows = slice(s * slab, (s + 1) * slab)
        epilogue(_dot(a_ref[rows, :], wb), rows, o_ref, *extras)


def _row_spec(tm, width):
    return pl.BlockSpec((tm, width), lambda i, j: (i, 0))


def _col_spec(tn, blk0=0):
    return pl.BlockSpec((1, tn), lambda i, j: (0, blk0 + j))


def _tile_spec(tm, tn, blk0=0):
    return pl.BlockSpec((tm, tn), lambda i, j: (i, blk0 + j))


def _linear(name, a, w, col0, n_cols, epilogue, extras=(), out_dtype=_BF16, tm=_TM, tn=_TN,
            transposed_out=False, row_splits=1):
    m, k = a.shape
    blk0 = col0 // tn
    arrays = [x for x, _ in extras]
    specs = [s for _, s in extras]
    if transposed_out:
        out_spec = pl.BlockSpec((tn, tm), lambda i, j: (j, i))
        out_shape = jax.ShapeDtypeStruct((n_cols, m), out_dtype)
    else:
        out_spec = pl.BlockSpec((tm, tn), lambda i, j: (i, j))
        out_shape = jax.ShapeDtypeStruct((m, n_cols), out_dtype)
    return pl.pallas_call(
        functools.partial(_linear_kernel, epilogue=epilogue, row_splits=row_splits),
        grid=(m // tm, n_cols // tn),
        in_specs=[
            pl.BlockSpec((tm, k), lambda i, j: (i, 0)),
            pl.BlockSpec((k, tn), lambda i, j: (0, blk0 + j)),
            *specs,
        ],
        out_specs=out_spec,
        out_shape=out_shape,
        compiler_params=_params(("arbitrary", "arbitrary")),
        name=name,
    )(a, w, *arrays)


def _transpose_epilogue(acc, rows, o_ref):
    o_ref[:, rows] = acc.T.astype(o_ref.dtype)


def _gelu_epilogue(acc, rows, o_ref):
    o_ref[rows, :] = jax.nn.gelu(acc).astype(o_ref.dtype)


def _sigmoid_epilogue(acc, rows, o_ref):
    o_ref[rows, :] = jax.nn.sigmoid(acc).astype(o_ref.dtype)


def _rope_epilogue(acc, rows, o_ref, cos_ref, sup_ref, sdn_ref, scale_ref):
    cos, sup, sdn = cos_ref[rows, :], sup_ref[rows, :], sdn_ref[rows, :]
    for c in range(acc.shape[1] // _LANES):
        sl = slice(c * _LANES, (c + 1) * _LANES)
        t = acc[:, sl]
        r = (t * cos
             + pltpu.roll(t, _ROPE_HALF, 1) * sup
             + pltpu.roll(t, _LANES - _ROPE_HALF, 1) * sdn)
        o_ref[rows, sl] = (r * scale_ref[:, sl]).astype(o_ref.dtype)


def _residual_epilogue(acc, rows, o_ref, x_ref):
    o_ref[rows, :] = _ALPHA * x_ref[rows, :] + acc


def _relu2_epilogue(acc, rows, o_ref, b_ref):
    z = jnp.maximum(acc + b_ref[...], 0.0)
    o_ref[rows, :] = (z * z).astype(o_ref.dtype)


def _attn_kernel(qi_tab, ki_tab, lq1_ref, lk1_ref, lq2_ref, lk2_ref, subw_ref,
                 q_ref, k_ref, vt_ref, *rest, n_cast, tq, tk, rq, ks, kh, ahead, lambda_init):
    cast_in = rest[:n_cast]
    o_ref = rest[n_cast]
    cast_out = rest[n_cast + 1:2 * n_cast + 1]
    m_ref, l_ref, acc_ref = rest[2 * n_cast + 1:]
    p_idx = pl.program_id(1)
    qi = qi_tab[p_idx]
    ki = ki_tab[p_idx]

    @pl.when(ki == 0)
    def _():
        m_ref[...] = jnp.full(m_ref.shape, -jnp.inf, _F32)
        l_ref[...] = jnp.zeros(l_ref.shape, _F32)
        acc_ref[...] = jnp.zeros(acc_ref.shape, _F32)

    def step(off):

        def span_off(b):
            return None if off is None else off + b * ks

        def keys_used(r, b):
            if off is None:
                return ks
            last_q = (r + 1) * rq - 1
            return kh * sum(1 for h in range(ks // kh) if span_off(b) + h * kh <= last_q)

        def needs_mask(r, b):
            return off is not None and span_off(b) + keys_used(r, b) - 1 > r * rq

        chunks = [(c, r, b) for b in range(tk // ks) for r in range(tq // rq) for c in range(2)
                  if keys_used(r, b) > 0]
        is_last_block = off is not None and off + tk >= tq
        last_chunk_of = {r: i for i, (_, r, _) in enumerate(chunks)}

        def scores(c, r, b):
            sl = slice(c * _HEAD_DIM, (c + 1) * _HEAD_DIM)
            q = q_ref[r * rq:(r + 1) * rq, sl]
            parts = [lax.dot_general(k_ref[b * ks + h * kh:b * ks + (h + 1) * kh, sl], q,
                                     (((1,), (1,)), ((), ())), preferred_element_type=_F32)
                     for h in range(keys_used(r, b) // kh)]
            return jnp.concatenate(parts, axis=0)

        def softmax_pv(c, r, b, st):
            qs = slice(r * rq, (r + 1) * rq)
            nk = keys_used(r, b)
            if needs_mask(r, b):
                delta = (lax.broadcasted_iota(jnp.int32, (nk, rq), 0)
                         - lax.broadcasted_iota(jnp.int32, (nk, rq), 1))
                st = jnp.where(delta <= r * rq - span_off(b), st, -jnp.inf)
            m_prev = m_ref[c, :, qs]
            m_new = jnp.maximum(m_prev, jnp.max(st, axis=0, keepdims=True))
            alpha = jnp.exp2(m_prev - m_new)
            p = jnp.exp2(st - m_new)
            l_ref[c, :, qs] = alpha * l_ref[c, :, qs] + jnp.sum(p, axis=0, keepdims=True)
            acc_ref[c, :, qs] = (alpha * acc_ref[c, :, qs]
                                 + _dot(vt_ref[:, b * ks:b * ks + nk], p.astype(_BF16)))
            m_ref[c, :, qs] = m_new

        pending = [scores(*ch) for ch in chunks[:ahead]]
        def cast_piece(i):
            for src, dst in zip(cast_in, cast_out):
                rows = src.shape[0] // _CAST_PIECES
                dst[i * rows:(i + 1) * rows, :] = src[i * rows:(i + 1) * rows, :].astype(dst.dtype)

        assert len(chunks) >= _CAST_PIECES
        if is_last_block:
            for r in range(tq // rq):
                if r not in last_chunk_of:
                    write_out(r)
        for i, ch in enumerate(chunks):
            if i + ahead < len(chunks):
                pending.append(scores(*chunks[i + ahead]))
            softmax_pv(*ch, pending.pop(0))
            if i < _CAST_PIECES:
                cast_piece(i)
            if is_last_block and last_chunk_of[ch[1]] == i:
                write_out(ch[1])

    def write_out(r):
        qs = slice(r * rq, (r + 1) * rq)
        lam = (jnp.exp(jnp.sum(lq1_ref[...] * lk1_ref[...], axis=1, keepdims=True))
               - jnp.exp(jnp.sum(lq2_ref[...] * lk2_ref[...], axis=1, keepdims=True))
               + lambda_init)
        ot = (acc_ref[0, :, qs] / l_ref[0, :, qs]
              - lam * (acc_ref[1, :, qs] / l_ref[1, :, qs]))
        ot = ot * lax.rsqrt(jnp.mean(ot * ot, axis=0, keepdims=True) + _LN_EPS)
        ot = ot * subw_ref[...] * (1.0 - lambda_init)
        o_ref[qs, :] = ot.T.astype(o_ref.dtype)

    off = ki * tk - qi * tq

    @pl.when(off + tk <= 0)
    def _():
        step(None)

    for diag_off in range(0, tq, tk):
        @pl.when(off == diag_off)
        def _(diag_off=diag_off):
            step(diag_off)


def _causal_pairs(n_q, tq, tk):
    qs, ks = [], []
    for qi in range(n_q):
        for ki in range(((qi + 1) * tq - 1) // tk + 1):
            qs.append(qi)
            ks.append(ki)
    return np.asarray(qs, np.int32), np.asarray(ks, np.int32)


def _attention(qk, vt, lq1, lk1, lq2, lk2, subw_col, lambda_init, cast_weights,
               tq=2048, tk=2048, rq=512, ks=1024, kh=512, ahead=2):
    s_len = qk.shape[0]
    q_blocks, k_blocks = _causal_pairs(s_len // tq, tq, tk)
    n_pairs = len(q_blocks)
    assert n_pairs >= _CAST_COL_BLOCKS
    k_blk0 = _SEG // _V_DIM
    vec = lambda n: pl.BlockSpec((1, n), lambda h, p, qt, kt: (0, 0))

    def cast_specs_of(w, col_lo, col_hi):
        rows, cols = w.shape[0] // _HEADS, (col_hi - col_lo) // _CAST_COL_BLOCKS
        blk0 = col_lo // cols
        band = lambda p: jnp.minimum(p, _CAST_COL_BLOCKS - 1)
        return (pl.BlockSpec((rows, cols), lambda h, p, qt, kt: (h, blk0 + band(p))),
                pl.BlockSpec((rows, cols), lambda h, p, qt, kt: (h, band(p))))

    cast_in_specs, cast_out_specs = zip(*[cast_specs_of(*cw) for cw in cast_weights])
    cast_arrays = [w for w, _, _ in cast_weights]
    kern = functools.partial(_attn_kernel, n_cast=len(cast_weights), tq=tq, tk=tk, rq=rq, ks=ks,
                             kh=kh, ahead=ahead, lambda_init=lambda_init)
    outs = pl.pallas_call(
        kern,
        grid_spec=pltpu.PrefetchScalarGridSpec(
            num_scalar_prefetch=2,
            grid=(_HEADS, n_pairs),
            in_specs=[
                vec(_HEAD_DIM), vec(_HEAD_DIM), vec(_HEAD_DIM), vec(_HEAD_DIM),
                pl.BlockSpec((_V_DIM, 1), lambda h, p, qt, kt: (0, 0)),
                pl.BlockSpec((tq, _V_DIM), lambda h, p, qt, kt: (qt[p], h)),
                pl.BlockSpec((tk, _V_DIM), lambda h, p, qt, kt: (kt[p], k_blk0 + h)),
                pl.BlockSpec((_V_DIM, tk), lambda h, p, qt, kt: (h, kt[p])),
                *cast_in_specs,
            ],
            out_specs=[pl.BlockSpec((tq, _V_DIM), lambda h, p, qt, kt: (qt[p], h)),
                       *cast_out_specs],
            scratch_shapes=[
                pltpu.VMEM((2, 1, tq), _F32),
                pltpu.VMEM((2, 1, tq), _F32),
                pltpu.VMEM((2, _V_DIM, tq), _F32),
            ],
        ),
        out_shape=[jax.ShapeDtypeStruct((s_len, _HEADS * _V_DIM), _BF16),
                   *[jax.ShapeDtypeStruct((w.shape[0], hi - lo), _BF16)
                     for w, lo, hi in cast_weights]],
        compiler_params=_params(("arbitrary", "arbitrary")),
        name="diff_attn",
    )(jnp.asarray(q_blocks), jnp.asarray(k_blocks), lq1, lk1, lq2, lk2, subw_col, qk, qk, vt,
      *cast_arrays)
    return outs[0], outs[1:]


def _sgu_kernel(u_ref, s_ref, g_ref, b_ref, w_ref, bs_ref, o_ref, *, tm):
    gd = _SEG // _SGU_GROUPS
    sn = _layer_norm_rows(s_ref[...].astype(_F32), g_ref[...], b_ref[...]).astype(_BF16)
    row = lax.broadcasted_iota(jnp.int32, (_CHUNK, _CHUNK), 0)
    col = lax.broadcasted_iota(jnp.int32, (_CHUNK, _CHUNK), 1)
    for g in range(_SGU_GROUPS):
        w = jnp.where(row >= col, w_ref[g], 0.0).astype(_BF16)
        bias = bs_ref[:, g:g + 1]
        cs = slice(g * gd, (g + 1) * gd)
        for c in range(tm // _CHUNK):
            rs = slice(c * _CHUNK, (c + 1) * _CHUNK)
            mixed = _dot(w, sn[rs, cs]) + bias
            o_ref[rs, cs] = (u_ref[rs, cs].astype(_F32) * mixed).astype(o_ref.dtype)


def _sgu(us, ln_g, ln_b, w_s, b_s_t, tm=512):
    s_len = us.shape[0]
    return pl.pallas_call(
        functools.partial(_sgu_kernel, tm=tm),
        grid=(s_len // tm,),
        in_specs=[
            pl.BlockSpec((tm, _SEG), lambda i: (i, 0)),
            pl.BlockSpec((tm, _SEG), lambda i: (i, 1)),
            pl.BlockSpec((1, _SEG), lambda i: (0, 0)),
            pl.BlockSpec((1, _SEG), lambda i: (0, 0)),
            pl.BlockSpec((_SGU_GROUPS, _CHUNK, _CHUNK), lambda i: (0, 0, 0)),
            pl.BlockSpec((_CHUNK, _SGU_GROUPS), lambda i: (0, 0)),
        ],
        out_specs=pl.BlockSpec((tm, _SEG), lambda i: (i, 0)),
        out_shape=jax.ShapeDtypeStruct((s_len, _SEG), _BF16),
        compiler_params=_params(("arbitrary",)),
        name="sgu",
    )(us, us, ln_g, ln_b, w_s, b_s_t)


def _merge_kernel(ya_ref, yb_ref, wa_ref, wb_ref, ga_ref, gb_ref, o_ref):
    wa = wa_ref[...].astype(_BF16)
    wb = wb_ref[...].astype(_BF16)
    slab = ya_ref.shape[0] // _ROW_SPLITS
    for s in range(_ROW_SPLITS):
        rows = slice(s * slab, (s + 1) * slab)
        a = _dot(ya_ref[rows, :], wa)
        b = _dot(yb_ref[rows, :], wb)
        o_ref[rows, :] = (ga_ref[rows, :].astype(_F32) * a
                          + gb_ref[rows, :].astype(_F32) * b).astype(o_ref.dtype)


def _merge(y_a, y_b, w_a, w_b, gates, tm=_TM, tn=_TN_BF16):
    m, k = y_a.shape
    n = w_a.shape[1]
    return pl.pallas_call(
        _merge_kernel,
        grid=(m // tm, n // tn),
        in_specs=[
            pl.BlockSpec((tm, k), lambda i, j: (i, 0)),
            pl.BlockSpec((tm, k), lambda i, j: (i, 0)),
            pl.BlockSpec((k, tn), lambda i, j: (0, j)),
            pl.BlockSpec((k, tn), lambda i, j: (0, j)),
            _tile_spec(tm, tn),
            _tile_spec(tm, tn, n // tn),
        ],
        out_specs=pl.BlockSpec((tm, tn), lambda i, j: (i, j)),
        out_shape=jax.ShapeDtypeStruct((m, n), _BF16),
        compiler_params=_params(("arbitrary", "arbitrary")),
        name="merge",
    )(y_a, y_b, w_a, w_b, gates, gates)


def _ln_bf16_kernel(x_ref, g_ref, b_ref, o_ref):
    o_ref[...] = _layer_norm_rows(x_ref[...], g_ref[...], b_ref[...]).astype(o_ref.dtype)


def _ln_bf16(x, g, b, tm=256):
    m, n = x.shape
    row = pl.BlockSpec((tm, n), lambda i: (i, 0))
    vec = pl.BlockSpec((1, n), lambda i: (0, 0))
    return pl.pallas_call(
        _ln_bf16_kernel,
        grid=(m // tm,),
        in_specs=[row, vec, vec],
        out_specs=row,
        out_shape=jax.ShapeDtypeStruct((m, n), _BF16),
        compiler_params=_params(("arbitrary",)),
        name="ln1",
    )(x, g, b)


def _final_kernel(pre_ref, ff_ref, g1_ref, b1_ref, bo_ref, g2_ref, b2_ref, o_ref):
    x = pre_ref[...]
    xc = x - jnp.mean(x, axis=-1, keepdims=True)
    var = jnp.mean(xc * xc, axis=-1, keepdims=True)
    shift = _ALPHA * b1_ref[...] + bo_ref[...]
    y = xc * (_ALPHA * lax.rsqrt(var + _LN_EPS)) * g1_ref[...] + (ff_ref[...] + shift)
    o_ref[...] = _layer_norm_rows(y, g2_ref[...], b2_ref[...])


def _final(pre1, ff, g1, b1, b_out, g2, b2, tm=256):
    m, n = pre1.shape
    row = pl.BlockSpec((tm, n), lambda i: (i, 0))
    vec = pl.BlockSpec((1, n), lambda i: (0, 0))
    return pl.pallas_call(
        _final_kernel,
        grid=(m // tm,),
        in_specs=[row, row, vec, vec, vec, vec, vec],
        out_specs=row,
        out_shape=jax.ShapeDtypeStruct((m, n), _F32),
        compiler_params=_params(("arbitrary",)),
        name="final_ln",
    )(pre1, ff, g1, b1, b_out, g2, b2)


def _mlp_out_kernel(z_ref, w_ref, o_ref):
    kk = pl.program_id(2)
    slab = z_ref.shape[0] // _ROW_SPLITS

    def accumulate(first):
        wb = w_ref[...].astype(_BF16)
        for s in range(_ROW_SPLITS):
            rows = slice(s * slab, (s + 1) * slab)
            part = _dot(z_ref[rows, :], wb)
            o_ref[rows, :] = part if first else o_ref[rows, :] + part

    @pl.when(kk == 0)
    def _():
        accumulate(True)

    @pl.when(kk > 0)
    def _():
        accumulate(False)


def _mlp_out(z, w, tm=2048, tn=1024, tk=2048):
    m, k = z.shape
    n = w.shape[1]
    return pl.pallas_call(
        _mlp_out_kernel,
        grid=(m // tm, n // tn, k // tk),
        in_specs=[
            pl.BlockSpec((tm, tk), lambda i, j, kk: (i, kk)),
            pl.BlockSpec((tk, tn), lambda i, j, kk: (kk, j)),
        ],
        out_specs=pl.BlockSpec((tm, tn), lambda i, j, kk: (i, j)),
        out_shape=jax.ShapeDtypeStruct((m, n), _F32),
        compiler_params=_params(("arbitrary", "arbitrary", "arbitrary")),
        name="mlp_out",
    )(z, w)


def _rope_tables(s_len):
    inv_freq = _ROPE_THETA ** (-np.arange(0, _ROPE_DIM, 2, dtype=np.float64) / _ROPE_DIM)
    ang = np.arange(s_len, dtype=np.float64)[:, None] * inv_freq[None, :]
    cos, sin = np.cos(ang), np.sin(ang)
    ones = np.ones((s_len, _LANES - _ROPE_DIM))
    zeros_h = np.zeros((s_len, _ROPE_HALF))
    zeros_r = np.zeros((s_len, _LANES - _ROPE_DIM))
    cos_t = np.concatenate([cos, cos, ones], axis=1)
    sin_up = np.concatenate([zeros_h, sin, zeros_r], axis=1)
    sin_dn = np.concatenate([-sin, zeros_h, zeros_r], axis=1)
    return tuple(jnp.asarray(t, dtype=_F32) for t in (cos_t, sin_up, sin_dn))


def kernel(x, w_in, lambda_q1, lambda_k1, lambda_q2, lambda_k2, subln_w, sgu_ln_g, sgu_ln_b,
           w_spatial, b_spatial, w_proj_attn, w_proj_sgu, w_out, ln1_g, ln1_b, w_mlp_in, b_mlp_in,
           w_mlp_out, b_mlp_out, ln2_g, ln2_b):
    bsz, s_len, d = x.shape
    assert bsz == 1 and w_in.shape[0] == _DEPTH == 1
    h = x.reshape(s_len, d)
    l = 0
    lambda_init = 0.8 - 0.6 * math.exp(-0.3 * l)
    cos_t, sin_up, sin_dn = _rope_tables(s_len)
    q_scale = _HEAD_DIM ** -0.5 * math.log2(math.e)
    qk_scale = jnp.concatenate([jnp.full((1, _SEG), q_scale, _F32), jnp.ones((1, _SEG), _F32)], axis=1)

    hb = _cast_bf16(h)
    w_in_l = w_in[l]
    qk = _linear("proj_qk", hb, w_in_l, 0, 2 * _SEG, _rope_epilogue,
                 extras=[(cos_t, _row_spec(_TM, _LANES)), (sin_up, _row_spec(_TM, _LANES)),
                         (sin_dn, _row_spec(_TM, _LANES)), (qk_scale, _col_spec(_TN))],
                 row_splits=_ROW_SPLITS)
    vt = _linear("proj_v", hb, w_in_l, 2 * _SEG, _SEG, _transpose_epilogue, transposed_out=True)

    whole = lambda w: (w, 0, w.shape[1])
    y_a, (w_tail, w_pa, w_ps, w_o, w_mi, w_mo) = _attention(
        qk, vt, lambda_q1[l][None], lambda_k1[l][None], lambda_q2[l][None], lambda_k2[l][None],
        subln_w[l][:, None], lambda_init,
        cast_weights=[(w_in_l, 3 * _SEG, w_in_l.shape[1]), whole(w_proj_attn[l]),
                      whole(w_proj_sgu[l]), whole(w_out[l]), whole(w_mlp_in[l]),
                      whole(w_mlp_out[l])])
    us = _linear("proj_us", hb, w_tail, 0, 2 * _SEG, _gelu_epilogue, tn=_TN_BF16)
    gates = _linear("proj_gates", hb, w_tail, 2 * _SEG, 2 * d, _sigmoid_epilogue, tn=_TN_BF16,
                    row_splits=_ROW_SPLITS)
    y_b = _sgu(us, sgu_ln_g[l][None], sgu_ln_b[l][None], w_spatial[l], b_spatial[l].T)
    merged = _merge(y_a, y_b, w_pa, w_ps, gates)
    pre1 = _linear("outproj", merged, w_o, 0, d, _residual_epilogue,
                   extras=[(h, _tile_spec(_TM, _TN_BF16))], out_dtype=_F32, tn=_TN_BF16)
    h1b = _ln_bf16(pre1, ln1_g[l][None], ln1_b[l][None])
    z = _linear("mlp_in", h1b, w_mi, 0, w_mi.shape[1], _relu2_epilogue,
                extras=[(b_mlp_in[l][None], _col_spec(_TN_BF16))], tn=_TN_BF16)
    ff = _mlp_out(z, w_mo)
    out = _final(pre1, ff, ln1_g[l][None], ln1_b[l][None], b_mlp_out[l][None],
                 ln2_g[l][None], ln2_b[l][None])
    return out.reshape(bsz, s_len, d)
```

```python
import functools
import math

import numpy as np
import jax
import jax.numpy as jnp
from jax import lax
from jax.experimental import pallas as pl
from jax.experimental.pallas import tpu as pltpu

_F32 = jnp.float32
_BF16 = jnp.bfloat16

_LANES = 128
_VMEM_LIMIT_BYTES = 60 * 1024 * 1024

_HEADS = 8
_HEAD_DIM = 128
_V_DIM = 2 * _HEAD_DIM
_ROPE_DIM = _HEAD_DIM // 4
_ROPE_HALF = _ROPE_DIM // 2
_ROPE_THETA = 500000.0
_SGU_GROUPS = 8
_CHUNK = 128
_DEPTH = 1
_ALPHA = (2.0 * _DEPTH) ** 0.25
_LN_EPS = 1e-5
_SEG = 2048

_TM = 1024
_TN = 512
_TN_BF16 = 1024
_ROW_SPLITS = 4
_CAST_COL_BLOCKS = 8
_CAST_PIECES = 8


def _params(sem):
    return pltpu.CompilerParams(dimension_semantics=sem, vmem_limit_bytes=_VMEM_LIMIT_BYTES)


def _dot(a, b):
    return jnp.dot(a, b, preferred_element_type=_F32)


def _layer_norm_rows(x, g, b):
    mu = jnp.mean(x, axis=-1, keepdims=True)
    xc = x - mu
    var = jnp.mean(xc * xc, axis=-1, keepdims=True)
    return xc * lax.rsqrt(var + _LN_EPS) * g + b


def _cast_kernel(x_ref, o_ref):
    o_ref[...] = x_ref[...].astype(o_ref.dtype)


def _cast_bf16(x, tm=512):
    m, n = x.shape
    return pl.pallas_call(
        _cast_kernel,
        grid=(m // tm,),
        in_specs=[pl.BlockSpec((tm, n), lambda i: (i, 0))],
        out_specs=pl.BlockSpec((tm, n), lambda i: (i, 0)),
        out_shape=jax.ShapeDtypeStruct((m, n), _BF16),
        compiler_params=_params(("arbitrary",)),
        name="cast_x",
    )(x)


def _linear_kernel(a_ref, w_ref, *rest, epilogue, row_splits, n_extra, n_cast):
    extras = rest[:n_extra]
    cast_in = rest[n_extra:n_extra + n_cast]
    o_ref = rest[n_extra + n_cast]
    cast_out = rest[n_extra + n_cast + 1:]
    wb = w_ref[...].astype(_BF16)
    slab = a_ref.shape[0] // row_splits
    for s in range(row_splits):
        rows = slice(s * slab, (s + 1) * slab)
        epilogue(_dot(a_ref[rows, :], wb), rows, o_ref, *extras)
    for src, dst in zip(cast_in, cast_out):
        dst[...] = src[...].astype(dst.dtype)


def _row_spec(tm, width):
    return pl.BlockSpec((tm, width), lambda i, j: (i, 0))


def _col_spec(tn, blk0=0):
    return pl.BlockSpec((1, tn), lambda i, j: (0, blk0 + j))


def _tile_spec(tm, tn, blk0=0):
    return pl.BlockSpec((tm, tn), lambda i, j: (i, blk0 + j))


def _linear(name, a, w, col0, n_cols, epilogue, extras=(), out_dtype=_BF16, tm=_TM, tn=_TN,
            transposed_out=False, row_splits=1, convert=()):
    m, k = a.shape
    blk0 = col0 // tn
    n_i, n_j = m // tm, n_cols // tn
    arrays = [x for x, _ in extras]
    specs = [s for _, s in extras]
    if transposed_out:
        out_spec = pl.BlockSpec((tn, tm), lambda i, j: (j, i))
        out_shape = jax.ShapeDtypeStruct((n_cols, m), out_dtype)
    else:
        out_spec = pl.BlockSpec((tm, tn), lambda i, j: (i, j))
        out_shape = jax.ShapeDtypeStruct((m, n_cols), out_dtype)
    band_specs = [pl.BlockSpec((c.shape[0] // (n_i * n_j), c.shape[1]),
                               lambda i, j: (i * n_j + j, 0)) for c in convert]
    outs = pl.pallas_call(
        functools.partial(_linear_kernel, epilogue=epilogue, row_splits=row_splits,
                          n_extra=len(extras), n_cast=len(convert)),
        grid=(n_i, n_j),
        in_specs=[
            pl.BlockSpec((tm, k), lambda i, j: (i, 0)),
            pl.BlockSpec((k, tn), lambda i, j: (0, blk0 + j)),
            *specs,
            *band_specs,
        ],
        out_specs=[out_spec, *band_specs],
        out_shape=[out_shape, *[jax.ShapeDtypeStruct(c.shape, _BF16) for c in convert]],
        compiler_params=_params(("arbitrary", "arbitrary")),
        name=name,
    )(a, w, *arrays, *convert)
    return (outs[0], outs[1:]) if convert else outs[0]


def _transpose_epilogue(acc, rows, o_ref):
    o_ref[:, rows] = acc.T.astype(o_ref.dtype)


def _gelu_epilogue(acc, rows, o_ref):
    o_ref[rows, :] = jax.nn.gelu(acc).astype(o_ref.dtype)


def _sigmoid_epilogue(acc, rows, o_ref):
    o_ref[rows, :] = jax.nn.sigmoid(acc).astype(o_ref.dtype)


def _rope_epilogue(acc, rows, o_ref, cos_ref, sup_ref, sdn_ref, scale_ref):
    cos, sup, sdn = cos_ref[rows, :], sup_ref[rows, :], sdn_ref[rows, :]
    for c in range(acc.shape[1] // _LANES):
        sl = slice(c * _LANES, (c + 1) * _LANES)
        t = acc[:, sl]
        r = (t * cos
             + pltpu.roll(t, _ROPE_HALF, 1) * sup
             + pltpu.roll(t, _LANES - _ROPE_HALF, 1) * sdn)
        o_ref[rows, sl] = (r * scale_ref[:, sl]).astype(o_ref.dtype)


def _residual_epilogue(acc, rows, o_ref, x_ref):
    o_ref[rows, :] = _ALPHA * x_ref[rows, :] + acc


def _relu2_epilogue(acc, rows, o_ref, b_ref):
    z = jnp.maximum(acc + b_ref[...], 0.0)
    o_ref[rows, :] = (z * z).astype(o_ref.dtype)


def _attn_kernel(qi_tab, ki_tab, lq1_ref, lk1_ref, lq2_ref, lk2_ref, subw_ref,
                 q_ref, k_ref, vt_ref, *rest, n_cast, tq, tk, rq, ks, kh, ahead, lambda_init):
    cast_in = rest[:n_cast]
    o_ref = rest[n_cast]
    cast_out = rest[n_cast + 1:2 * n_cast + 1]
    m_ref, l_ref, acc_ref = rest[2 * n_cast + 1:]
    p_idx = pl.program_id(1)
    qi = qi_tab[p_idx]
    ki = ki_tab[p_idx]

    @pl.when(ki == 0)
    def _():
        m_ref[...] = jnp.full(m_ref.shape, -jnp.inf, _F32)
        l_ref[...] = jnp.zeros(l_ref.shape, _F32)
        acc_ref[...] = jnp.zeros(acc_ref.shape, _F32)

    def step(off):

        def span_off(b):
            return None if off is None else off + b * ks

        def keys_used(r, b):
            if off is None:
                return ks
            last_q = (r + 1) * rq - 1
            return kh * sum(1 for h in range(ks // kh) if span_off(b) + h * kh <= last_q)

        def needs_mask(r, b):
            return off is not None and span_off(b) + keys_used(r, b) - 1 > r * rq

        chunks = [(c, r, b) for b in range(tk // ks) for r in range(tq // rq) for c in range(2)
                  if keys_used(r, b) > 0]
        is_last_block = off is not None and off + tk >= tq
        last_chunk_of = {r: i for i, (_, r, _) in enumerate(chunks)}

        def scores(c, r, b):
            sl = slice(c * _HEAD_DIM, (c + 1) * _HEAD_DIM)
            q = q_ref[r * rq:(r + 1) * rq, sl]
            parts = [lax.dot_general(k_ref[b * ks + h * kh:b * ks + (h + 1) * kh, sl], q,
                                     (((1,), (1,)), ((), ())), preferred_element_type=_F32)
                     for h in range(keys_used(r, b) // kh)]
            return jnp.concatenate(parts, axis=0)

        def softmax_pv(c, r, b, st):
            qs = slice(r * rq, (r + 1) * rq)
            nk = keys_used(r, b)
            if needs_mask(r, b):
                delta = (lax.broadcasted_iota(jnp.int32, (nk, rq), 0)
                         - lax.broadcasted_iota(jnp.int32, (nk, rq), 1))
                st = jnp.where(delta <= r * rq - span_off(b), st, -jnp.inf)
            m_prev = m_ref[c, :, qs]
            m_new = jnp.maximum(m_prev, jnp.max(st, axis=0, keepdims=True))
            alpha = jnp.exp2(m_prev - m_new)
            p = jnp.exp2(st - m_new)
            l_ref[c, :, qs] = alpha * l_ref[c, :, qs] + jnp.sum(p, axis=0, keepdims=True)
            acc_ref[c, :, qs] = (alpha * acc_ref[c, :, qs]
                                 + _dot(vt_ref[:, b * ks:b * ks + nk], p.astype(_BF16)))
            m_ref[c, :, qs] = m_new

        pending = [scores(*ch) for ch in chunks[:ahead]]
        def cast_piece(i):
            for src, dst in zip(cast_in, cast_out):
                rows = src.shape[0] // _CAST_PIECES
                dst[i * rows:(i + 1) * rows, :] = src[i * rows:(i + 1) * rows, :].astype(dst.dtype)

        assert len(chunks) >= _CAST_PIECES
        if is_last_block:
            for r in range(tq // rq):
                if r not in last_chunk_of:
                    write_out(r)
        for i, ch in enumerate(chunks):
            if i + ahead < len(chunks):
                pending.append(scores(*chunks[i + ahead]))
            softmax_pv(*ch, pending.pop(0))
            if i < _CAST_PIECES:
                cast_piece(i)
            if is_last_block and last_chunk_of[ch[1]] == i:
                write_out(ch[1])

    def write_out(r):
        qs = slice(r * rq, (r + 1) * rq)
        lam = (jnp.exp(jnp.sum(lq1_ref[...] * lk1_ref[...], axis=1, keepdims=True))
               - jnp.exp(jnp.sum(lq2_ref[...] * lk2_ref[...], axis=1, keepdims=True))
               + lambda_init)
        ot = (acc_ref[0, :, qs] / l_ref[0, :, qs]
              - lam * (acc_ref[1, :, qs] / l_ref[1, :, qs]))
        ot = ot * lax.rsqrt(jnp.mean(ot * ot, axis=0, keepdims=True) + _LN_EPS)
        ot = ot * subw_ref[...] * (1.0 - lambda_init)
        o_ref[qs, :] = ot.T.astype(o_ref.dtype)

    off = ki * tk - qi * tq

    @pl.when(off + tk <= 0)
    def _():
        step(None)

    for diag_off in range(0, tq, tk):
        @pl.when(off == diag_off)
        def _(diag_off=diag_off):
            step(diag_off)


def _causal_pairs(n_q, tq, tk):
    qs, ks = [], []
    for qi in range(n_q):
        for ki in range(((qi + 1) * tq - 1) // tk + 1):
            qs.append(qi)
            ks.append(ki)
    return np.asarray(qs, np.int32), np.asarray(ks, np.int32)


def _attention(qk, vt, lq1, lk1, lq2, lk2, subw_col, lambda_init, cast_weights,
               tq=2048, tk=2048, rq=512, ks=1024, kh=512, ahead=2):
    s_len = qk.shape[0]
    q_blocks, k_blocks = _causal_pairs(s_len // tq, tq, tk)
    n_pairs = len(q_blocks)
    assert n_pairs >= _CAST_COL_BLOCKS
    k_blk0 = _SEG // _V_DIM
    vec = lambda n: pl.BlockSpec((1, n), lambda h, p, qt, kt: (0, 0))

    def cast_specs_of(w, col_lo, col_hi):
        rows, cols = w.shape[0] // _HEADS, (col_hi - col_lo) // _CAST_COL_BLOCKS
        blk0 = col_lo // cols
        band = lambda p: jnp.minimum(p, _CAST_COL_BLOCKS - 1)
        return (pl.BlockSpec((rows, cols), lambda h, p, qt, kt: (h, blk0 + band(p))),
                pl.BlockSpec((rows, cols), lambda h, p, qt, kt: (h, band(p))))

    cast_in_specs, cast_out_specs = zip(*[cast_specs_of(*cw) for cw in cast_weights])
    cast_arrays = [w for w, _, _ in cast_weights]
    kern = functools.partial(_attn_kernel, n_cast=len(cast_weights), tq=tq, tk=tk, rq=rq, ks=ks,
                             kh=kh, ahead=ahead, lambda_init=lambda_init)
    outs = pl.pallas_call(
        kern,
        grid_spec=pltpu.PrefetchScalarGridSpec(
            num_scalar_prefetch=2,
            grid=(_HEADS, n_pairs),
            in_specs=[
                vec(_HEAD_DIM), vec(_HEAD_DIM), vec(_HEAD_DIM), vec(_HEAD_DIM),
                pl.BlockSpec((_V_DIM, 1), lambda h, p, qt, kt: (0, 0)),
                pl.BlockSpec((tq, _V_DIM), lambda h, p, qt, kt: (qt[p], h)),
                pl.BlockSpec((tk, _V_DIM), lambda h, p, qt, kt: (kt[p], k_blk0 + h)),
                pl.BlockSpec((_V_DIM, tk), lambda h, p, qt, kt: (h, kt[p])),
                *cast_in_specs,
            ],
            out_specs=[pl.BlockSpec((tq, _V_DIM), lambda h, p, qt, kt: (qt[p], h)),
                       *cast_out_specs],
            scratch_shapes=[
                pltpu.VMEM((2, 1, tq), _F32),
                pltpu.VMEM((2, 1, tq), _F32),
                pltpu.VMEM((2, _V_DIM, tq), _F32),
            ],
        ),
        out_shape=[jax.ShapeDtypeStruct((s_len, _HEADS * _V_DIM), _BF16),
                   *[jax.ShapeDtypeStruct((w.shape[0], hi - lo), _BF16)
                     for w, lo, hi in cast_weights]],
        compiler_params=_params(("arbitrary", "arbitrary")),
        name="diff_attn",
    )(jnp.asarray(q_blocks), jnp.asarray(k_blocks), lq1, lk1, lq2, lk2, subw_col, qk, qk, vt,
      *cast_arrays)
    return outs[0], outs[1:]


def _sgu_kernel(u_ref, s_ref, g_ref, b_ref, w_ref, bs_ref, o_ref, *, tm):
    gd = _SEG // _SGU_GROUPS
    sn = _layer_norm_rows(s_ref[...].astype(_F32), g_ref[...], b_ref[...]).astype(_BF16)
    row = lax.broadcasted_iota(jnp.int32, (_CHUNK, _CHUNK), 0)
    col = lax.broadcasted_iota(jnp.int32, (_CHUNK, _CHUNK), 1)
    for g in range(_SGU_GROUPS):
        w = jnp.where(row >= col, w_ref[g], 0.0).astype(_BF16)
        bias = bs_ref[:, g:g + 1]
        cs = slice(g * gd, (g + 1) * gd)
        for c in range(tm // _CHUNK):
            rs = slice(c * _CHUNK, (c + 1) * _CHUNK)
            mixed = _dot(w, sn[rs, cs]) + bias
            o_ref[rs, cs] = (u_ref[rs, cs].astype(_F32) * mixed).astype(o_ref.dtype)


def _sgu(us, ln_g, ln_b, w_s, b_s_t, tm=512):
    s_len = us.shape[0]
    return pl.pallas_call(
        functools.partial(_sgu_kernel, tm=tm),
        grid=(s_len // tm,),
        in_specs=[
            pl.BlockSpec((tm, _SEG), lambda i: (i, 0)),
            pl.BlockSpec((tm, _SEG), lambda i: (i, 1)),
            pl.BlockSpec((1, _SEG), lambda i: (0, 0)),
            pl.BlockSpec((1, _SEG), lambda i: (0, 0)),
            pl.BlockSpec((_SGU_GROUPS, _CHUNK, _CHUNK), lambda i: (0, 0, 0)),
            pl.BlockSpec((_CHUNK, _SGU_GROUPS), lambda i: (0, 0)),
        ],
        out_specs=pl.BlockSpec((tm, _SEG), lambda i: (i, 0)),
        out_shape=jax.ShapeDtypeStruct((s_len, _SEG), _BF16),
        compiler_params=_params(("arbitrary",)),
        name="sgu",
    )(us, us, ln_g, ln_b, w_s, b_s_t)


def _merge_kernel(ya_ref, yb_ref, wa_ref, wb_ref, ga_ref, gb_ref, o_ref):
    wa = wa_ref[...].astype(_BF16)
    wb = wb_ref[...].astype(_BF16)
    slab = ya_ref.shape[0] // _ROW_SPLITS
    for s in range(_ROW_SPLITS):
        rows = slice(s * slab, (s + 1) * slab)
        a = _dot(ya_ref[rows, :], wa)
        b = _dot(yb_ref[rows, :], wb)
        o_ref[rows, :] = (ga_ref[rows, :].astype(_F32) * a
                          + gb_ref[rows, :].astype(_F32) * b).astype(o_ref.dtype)


def _merge(y_a, y_b, w_a, w_b, gates, tm=_TM, tn=_TN_BF16):
    m, k = y_a.shape
    n = w_a.shape[1]
    return pl.pallas_call(
        _merge_kernel,
        grid=(m // tm, n // tn),
        in_specs=[
            pl.BlockSpec((tm, k), lambda i, j: (i, 0)),
            pl.BlockSpec((tm, k), lambda i, j: (i, 0)),
            pl.BlockSpec((k, tn), lambda i, j: (0, j)),
            pl.BlockSpec((k, tn), lambda i, j: (0, j)),
            _tile_spec(tm, tn),
            _tile_spec(tm, tn, n // tn),
        ],
        out_specs=pl.BlockSpec((tm, tn), lambda i, j: (i, j)),
        out_shape=jax.ShapeDtypeStruct((m, n), _BF16),
        compiler_params=_params(("arbitrary", "arbitrary")),
        name="merge",
    )(y_a, y_b, w_a, w_b, gates, gates)


def _ln_bf16_kernel(x_ref, g_ref, b_ref, o_ref):
    o_ref[...] = _layer_norm_rows(x_ref[...], g_ref[...], b_ref[...]).astype(o_ref.dtype)


def _ln_bf16(x, g, b, tm=256):
    m, n = x.shape
    row = pl.BlockSpec((tm, n), lambda i: (i, 0))
    vec = pl.BlockSpec((1, n), lambda i: (0, 0))
    return pl.pallas_call(
        _ln_bf16_kernel,
        grid=(m // tm,),
        in_specs=[row, vec, vec],
        out_specs=row,
        out_shape=jax.ShapeDtypeStruct((m, n), _BF16),
        compiler_params=_params(("arbitrary",)),
        name="ln1",
    )(x, g, b)


def _final_kernel(pre_ref, ff_ref, g1_ref, b1_ref, bo_ref, g2_ref, b2_ref, o_ref):
    x = pre_ref[...]
    xc = x - jnp.mean(x, axis=-1, keepdims=True)
    var = jnp.mean(xc * xc, axis=-1, keepdims=True)
    shift = _ALPHA * b1_ref[...] + bo_ref[...]
    y = xc * (_ALPHA * lax.rsqrt(var + _LN_EPS)) * g1_ref[...] + (ff_ref[...] + shift)
    o_ref[...] = _layer_norm_rows(y, g2_ref[...], b2_ref[...])


def _final(pre1, ff, g1, b1, b_out, g2, b2, tm=256):
    m, n = pre1.shape
    row = pl.BlockSpec((tm, n), lambda i: (i, 0))
    vec = pl.BlockSpec((1, n), lambda i: (0, 0))
    return pl.pallas_call(
        _final_kernel,
        grid=(m // tm,),
        in_specs=[row, row, vec, vec, vec, vec, vec],
        out_specs=row,
        out_shape=jax.ShapeDtypeStruct((m, n), _F32),
        compiler_params=_params(("arbitrary",)),
        name="final_ln",
    )(pre1, ff, g1, b1, b_out, g2, b2)


def _mlp_out_kernel(z_ref, w_ref, o_ref):
    kk = pl.program_id(2)
    slab = z_ref.shape[0] // _ROW_SPLITS

    def accumulate(first):
        wb = w_ref[...].astype(_BF16)
        for s in range(_ROW_SPLITS):
            rows = slice(s * slab, (s + 1) * slab)
            part = _dot(z_ref[rows, :], wb)
            o_ref[rows, :] = part if first else o_ref[rows, :] + part

    @pl.when(kk == 0)
    def _():
        accumulate(True)

    @pl.when(kk > 0)
    def _():
        accumulate(False)


def _mlp_out(z, w, tm=2048, tn=1024, tk=2048):
    m, k = z.shape
    n = w.shape[1]
    return pl.pallas_call(
        _mlp_out_kernel,
        grid=(m // tm, n // tn, k // tk),
        in_specs=[
            pl.BlockSpec((tm, tk), lambda i, j, kk: (i, kk)),
            pl.BlockSpec((tk, tn), lambda i, j, kk: (kk, j)),
        ],
        out_specs=pl.BlockSpec((tm, tn), lambda i, j, kk: (i, j)),
        out_shape=jax.ShapeDtypeStruct((m, n), _F32),
        compiler_params=_params(("arbitrary", "arbitrary", "arbitrary")),
        name="mlp_out",
    )(z, w)


def _rope_tables(s_len):
    inv_freq = _ROPE_THETA ** (-np.arange(0, _ROPE_DIM, 2, dtype=np.float64) / _ROPE_DIM)
    ang = np.arange(s_len, dtype=np.float64)[:, None] * inv_freq[None, :]
    cos, sin = np.cos(ang), np.sin(ang)
    ones = np.ones((s_len, _LANES - _ROPE_DIM))
    zeros_h = np.zeros((s_len, _ROPE_HALF))
    zeros_r = np.zeros((s_len, _LANES - _ROPE_DIM))
    cos_t = np.concatenate([cos, cos, ones], axis=1)
    sin_up = np.concatenate([zeros_h, sin, zeros_r], axis=1)
    sin_dn = np.concatenate([-sin, zeros_h, zeros_r], axis=1)
    return tuple(jnp.asarray(t, dtype=_F32) for t in (cos_t, sin_up, sin_dn))


def kernel(x, w_in, lambda_q1, lambda_k1, lambda_q2, lambda_k2, subln_w, sgu_ln_g, sgu_ln_b,
           w_spatial, b_spatial, w_proj_attn, w_proj_sgu, w_out, ln1_g, ln1_b, w_mlp_in, b_mlp_in,
           w_mlp_out, b_mlp_out, ln2_g, ln2_b):
    bsz, s_len, d = x.shape
    assert bsz == 1 and w_in.shape[0] == _DEPTH == 1
    h = x.reshape(s_len, d)
    l = 0
    lambda_init = 0.8 - 0.6 * math.exp(-0.3 * l)
    cos_t, sin_up, sin_dn = _rope_tables(s_len)
    q_scale = _HEAD_DIM ** -0.5 * math.log2(math.e)
    qk_scale = jnp.concatenate([jnp.full((1, _SEG), q_scale, _F32), jnp.ones((1, _SEG), _F32)], axis=1)

    hb = _cast_bf16(h)
    w_in_l = w_in[l]
    qk = _linear("proj_qk", hb, w_in_l, 0, 2 * _SEG, _rope_epilogue,
                 extras=[(cos_t, _row_spec(_TM, _LANES)), (sin_up, _row_spec(_TM, _LANES)),
                         (sin_dn, _row_spec(_TM, _LANES)), (qk_scale, _col_spec(_TN))],
                 row_splits=_ROW_SPLITS)
    vt = _linear("proj_v", hb, w_in_l, 2 * _SEG, _SEG, _transpose_epilogue, transposed_out=True)

    y_a, (w_tail,) = _attention(
        qk, vt, lambda_q1[l][None], lambda_k1[l][None], lambda_q2[l][None], lambda_k2[l][None],
        subln_w[l][:, None], lambda_init, cast_weights=[(w_in_l, 3 * _SEG, w_in_l.shape[1])])
    us, (w_pa, w_ps, w_o) = _linear("proj_us", hb, w_tail, 0, 2 * _SEG, _gelu_epilogue,
                                    tn=_TN_BF16,
                                    convert=[w_proj_attn[l], w_proj_sgu[l], w_out[l]])
    gates, (w_mi,) = _linear("proj_gates", hb, w_tail, 2 * _SEG, 2 * d, _sigmoid_epilogue,
                             tn=_TN_BF16, row_splits=_ROW_SPLITS, convert=[w_mlp_in[l]])
    y_b = _sgu(us, sgu_ln_g[l][None], sgu_ln_b[l][None], w_spatial[l], b_spatial[l].T)
    merged = _merge(y_a, y_b, w_pa, w_ps, gates)
    pre1 = _linear("outproj", merged, w_o, 0, d, _residual_epilogue,
                   extras=[(h, _tile_spec(_TM, _TN_BF16))], out_dtype=_F32, tn=_TN_BF16)
    h1b = _ln_bf16(pre1, ln1_g[l][None], ln1_b[l][None])
    z, (w_mo,) = _linear("mlp_in", h1b, w_mi, 0, w_mi.shape[1], _relu2_epilogue,
                         extras=[(b_mlp_in[l][None], _col_spec(_TN_BF16))], tn=_TN_BF16,
                         convert=[w_mlp_out[l]])
    ff = _mlp_out(z, w_mo)
    out = _final(pre1, ff, ln1_g[l][None], ln1_b[l][None], b_mlp_out[l][None],
                 ln2_g[l][None], ln2_b[l][None])
    return out.reshape(bsz, s_len, d)
```

```python
import functools
import math

import numpy as np
import jax
import jax.numpy as jnp
from jax import lax
from jax.experimental import pallas as pl
from jax.experimental.pallas import tpu as pltpu

_F32 = jnp.float32
_BF16 = jnp.bfloat16

_LANES = 128
_VMEM_LIMIT_BYTES = 60 * 1024 * 1024

_HEADS = 8
_HEAD_DIM = 128
_V_DIM = 2 * _HEAD_DIM
_ROPE_DIM = _HEAD_DIM // 4
_ROPE_HALF = _ROPE_DIM // 2
_ROPE_THETA = 500000.0
_SGU_GROUPS = 8
_CHUNK = 128
_DEPTH = 1
_ALPHA = (2.0 * _DEPTH) ** 0.25
_LN_EPS = 1e-5
_SEG = 2048

_TM = 1024
_TN = 512
_TN_BF16 = 1024
_ROW_SPLITS = 4
_CAST_COL_BLOCKS = 8
_CAST_PIECES = 8


def _params(sem):
    return pltpu.CompilerParams(dimension_semantics=sem, vmem_limit_bytes=_VMEM_LIMIT_BYTES)


def _dot(a, b):
    return jnp.dot(a, b, preferred_element_type=_F32)


def _layer_norm_rows(x, g, b):
    mu = jnp.mean(x, axis=-1, keepdims=True)
    xc = x - mu
    var = jnp.mean(xc * xc, axis=-1, keepdims=True)
    return xc * lax.rsqrt(var + _LN_EPS) * g + b


def _cast_kernel(x_ref, o_ref):
    o_ref[...] = x_ref[...].astype(o_ref.dtype)


def _cast_bf16(x, tm=512):
    m, n = x.shape
    return pl.pallas_call(
        _cast_kernel,
        grid=(m // tm,),
        in_specs=[pl.BlockSpec((tm, n), lambda i: (i, 0))],
        out_specs=pl.BlockSpec((tm, n), lambda i: (i, 0)),
        out_shape=jax.ShapeDtypeStruct((m, n), _BF16),
        compiler_params=_params(("arbitrary",)),
        name="cast_x",
    )(x)


def _linear_kernel(a_ref, w_ref, *rest, epilogue, row_splits, n_extra, n_cast):
    extras = rest[:n_extra]
    cast_in = rest[n_extra:n_extra + n_cast]
    o_ref = rest[n_extra + n_cast]
    cast_out = rest[n_extra + n_cast + 1:]
    wb = w_ref[...].astype(_BF16)
    slab = a_ref.shape[0] // row_splits
    for s in range(row_splits):
        rows = slice(s * slab, (s + 1) * slab)
        epilogue(_dot(a_ref[rows, :], wb), rows, o_ref, *extras)
    for src, dst in zip(cast_in, cast_out):
        dst[...] = src[...].astype(dst.dtype)


def _row_spec(tm, width):
    return pl.BlockSpec((tm, width), lambda i, j: (i, 0))


def _col_spec(tn, blk0=0):
    return pl.BlockSpec((1, tn), lambda i, j: (0, blk0 + j))


def _tile_spec(tm, tn, blk0=0):
    return pl.BlockSpec((tm, tn), lambda i, j: (i, blk0 + j))


def _linear(name, a, w, col0, n_cols, epilogue, extras=(), out_dtype=_BF16, tm=_TM, tn=_TN,
            transposed_out=False, row_splits=1, convert=()):
    m, k = a.shape
    blk0 = col0 // tn
    n_i, n_j = m // tm, n_cols // tn
    arrays = [x for x, _ in extras]
    specs = [s for _, s in extras]
    if transposed_out:
        out_spec = pl.BlockSpec((tn, tm), lambda i, j: (j, i))
        out_shape = jax.ShapeDtypeStruct((n_cols, m), out_dtype)
    else:
        out_spec = pl.BlockSpec((tm, tn), lambda i, j: (i, j))
        out_shape = jax.ShapeDtypeStruct((m, n_cols), out_dtype)
    band_specs = [pl.BlockSpec((c.shape[0] // (n_i * n_j), c.shape[1]),
                               lambda i, j: (i * n_j + j, 0)) for c in convert]
    outs = pl.pallas_call(
        functools.partial(_linear_kernel, epilogue=epilogue, row_splits=row_splits,
                          n_extra=len(extras), n_cast=len(convert)),
        grid=(n_i, n_j),
        in_specs=[
            pl.BlockSpec((tm, k), lambda i, j: (i, 0)),
            pl.BlockSpec((k, tn), lambda i, j: (0, blk0 + j)),
            *specs,
            *band_specs,
        ],
        out_specs=[out_spec, *band_specs],
        out_shape=[out_shape, *[jax.ShapeDtypeStruct(c.shape, _BF16) for c in convert]],
        compiler_params=_params(("arbitrary", "arbitrary")),
        name=name,
    )(a, w, *arrays, *convert)
    return (outs[0], outs[1:]) if convert else outs[0]


def _transpose_epilogue(acc, rows, o_ref):
    o_ref[:, rows] = acc.T.astype(o_ref.dtype)


def _gelu_epilogue(acc, rows, o_ref):
    o_ref[rows, :] = jax.nn.gelu(acc).astype(o_ref.dtype)


def _sigmoid_epilogue(acc, rows, o_ref):
    o_ref[rows, :] = jax.nn.sigmoid(acc).astype(o_ref.dtype)


def _rope_epilogue(acc, rows, o_ref, cos_ref, sup_ref, sdn_ref, scale_ref):
    cos, sup, sdn = cos_ref[rows, :], sup_ref[rows, :], sdn_ref[rows, :]
    for c in range(acc.shape[1] // _LANES):
        sl = slice(c * _LANES, (c + 1) * _LANES)
        t = acc[:, sl]
        r = (t * cos
             + pltpu.roll(t, _ROPE_HALF, 1) * sup
             + pltpu.roll(t, _LANES - _ROPE_HALF, 1) * sdn)
        o_ref[rows, sl] = (r * scale_ref[:, sl]).astype(o_ref.dtype)


def _residual_epilogue(acc, rows, o_ref, x_ref):
    o_ref[rows, :] = _ALPHA * x_ref[rows, :] + acc


def _relu2_epilogue(acc, rows, o_ref, b_ref):
    z = jnp.maximum(acc + b_ref[...], 0.0)
    o_ref[rows, :] = (z * z).astype(o_ref.dtype)


def _attn_kernel(qi_tab, ki_tab, lq1_ref, lk1_ref, lq2_ref, lk2_ref, subw_ref,
                 q_ref, k_ref, vt_ref, *rest, n_cast, tq, tk, rq, ks, kh, ahead, lambda_init):
    cast_in = rest[:n_cast]
    o_ref = rest[n_cast]
    cast_out = rest[n_cast + 1:2 * n_cast + 1]
    m_ref, l_ref, acc_ref = rest[2 * n_cast + 1:]
    p_idx = pl.program_id(1)
    qi = qi_tab[p_idx]
    ki = ki_tab[p_idx]

    @pl.when(ki == 0)
    def _():
        m_ref[...] = jnp.full(m_ref.shape, -jnp.inf, _F32)
        l_ref[...] = jnp.zeros(l_ref.shape, _F32)
        acc_ref[...] = jnp.zeros(acc_ref.shape, _F32)

    def step(off):

        def span_off(b):
            return None if off is None else off + b * ks

        def keys_used(r, b):
            if off is None:
                return ks
            last_q = (r + 1) * rq - 1
            return kh * sum(1 for h in range(ks // kh) if span_off(b) + h * kh <= last_q)

        def needs_mask(r, b):
            return off is not None and span_off(b) + keys_used(r, b) - 1 > r * rq

        chunks = [(c, r, b) for b in range(tk // ks) for r in range(tq // rq) for c in range(2)
                  if keys_used(r, b) > 0]
        is_last_block = off is not None and off + tk >= tq
        last_chunk_of = {r: i for i, (_, r, _) in enumerate(chunks)}

        def scores(c, r, b):
            sl = slice(c * _HEAD_DIM, (c + 1) * _HEAD_DIM)
            q = q_ref[r * rq:(r + 1) * rq, sl]
            parts = [lax.dot_general(k_ref[b * ks + h * kh:b * ks + (h + 1) * kh, sl], q,
                                     (((1,), (1,)), ((), ())), preferred_element_type=_F32)
                     for h in range(keys_used(r, b) // kh)]
            return jnp.concatenate(parts, axis=0)

        def softmax_pv(c, r, b, st):
            qs = slice(r * rq, (r + 1) * rq)
            nk = keys_used(r, b)
            if needs_mask(r, b):
                delta = (lax.broadcasted_iota(jnp.int32, (nk, rq), 0)
                         - lax.broadcasted_iota(jnp.int32, (nk, rq), 1))
                st = jnp.where(delta <= r * rq - span_off(b), st, -jnp.inf)
            m_prev = m_ref[c, :, qs]
            m_new = jnp.maximum(m_prev, jnp.max(st, axis=0, keepdims=True))
            alpha = jnp.exp2(m_prev - m_new)
            p = jnp.exp2(st - m_new)
            l_ref[c, :, qs] = alpha * l_ref[c, :, qs] + jnp.sum(p, axis=0, keepdims=True)
            acc_ref[c, :, qs] = (alpha * acc_ref[c, :, qs]
                                 + _dot(vt_ref[:, b * ks:b * ks + nk], p.astype(_BF16)))
            m_ref[c, :, qs] = m_new

        pending = [scores(*ch) for ch in chunks[:ahead]]
        def cast_piece(i):
            for src, dst in zip(cast_in, cast_out):
                rows = src.shape[0] // _CAST_PIECES
                dst[i * rows:(i + 1) * rows, :] = src[i * rows:(i + 1) * rows, :].astype(dst.dtype)

        assert len(chunks) >= _CAST_PIECES
        if is_last_block:
            for r in range(tq // rq):
                if r not in last_chunk_of:
                    write_out(r)
        for i, ch in enumerate(chunks):
            if i + ahead < len(chunks):
                pending.append(scores(*chunks[i + ahead]))
            softmax_pv(*ch, pending.pop(0))
            if i < _CAST_PIECES:
                cast_piece(i)
            if is_last_block and last_chunk_of[ch[1]] == i:
                write_out(ch[1])

    def write_out(r):
        qs = slice(r * rq, (r + 1) * rq)
        lam = (jnp.exp(jnp.sum(lq1_ref[...] * lk1_ref[...], axis=1, keepdims=True))
               - jnp.exp(jnp.sum(lq2_ref[...] * lk2_ref[...], axis=1, keepdims=True))
               + lambda_init)
        ot = (acc_ref[0, :, qs] / l_ref[0, :, qs]
              - lam * (acc_ref[1, :, qs] / l_ref[1, :, qs]))
        ot = ot * lax.rsqrt(jnp.mean(ot * ot, axis=0, keepdims=True) + _LN_EPS)
        ot = ot * subw_ref[...] * (1.0 - lambda_init)
        o_ref[qs, :] = ot.T.astype(o_ref.dtype)

    off = ki * tk - qi * tq

    @pl.when(off + tk <= 0)
    def _():
        step(None)

    for diag_off in range(0, tq, tk):
        @pl.when(off == diag_off)
        def _(diag_off=diag_off):
            step(diag_off)


def _causal_pairs(n_q, tq, tk):
    qs, ks = [], []
    for qi in range(n_q):
        for ki in range(((qi + 1) * tq - 1) // tk + 1):
            qs.append(qi)
            ks.append(ki)
    return np.asarray(qs, np.int32), np.asarray(ks, np.int32)


def _attention(qk, vt, lq1, lk1, lq2, lk2, subw_col, lambda_init, cast_weights,
               tq=2048, tk=2048, rq=512, ks=1024, kh=512, ahead=2):
    s_len = qk.shape[0]
    q_blocks, k_blocks = _causal_pairs(s_len // tq, tq, tk)
    n_pairs = len(q_blocks)
    assert n_pairs >= _CAST_COL_BLOCKS
    k_blk0 = _SEG // _V_DIM
    vec = lambda n: pl.BlockSpec((1, n), lambda h, p, qt, kt: (0, 0))

    def cast_specs_of(w, col_lo, col_hi):
        rows, cols = w.shape[0] // _HEADS, (col_hi - col_lo) // _CAST_COL_BLOCKS
        blk0 = col_lo // cols
        band = lambda p: jnp.minimum(p, _CAST_COL_BLOCKS - 1)
        return (pl.BlockSpec((rows, cols), lambda h, p, qt, kt: (h, blk0 + band(p))),
                pl.BlockSpec((rows, cols), lambda h, p, qt, kt: (h, band(p))))

    cast_in_specs, cast_out_specs = zip(*[cast_specs_of(*cw) for cw in cast_weights])
    cast_arrays = [w for w, _, _ in cast_weights]
    kern = functools.partial(_attn_kernel, n_cast=len(cast_weights), tq=tq, tk=tk, rq=rq, ks=ks,
                             kh=kh, ahead=ahead, lambda_init=lambda_init)
    outs = pl.pallas_call(
        kern,
        grid_spec=pltpu.PrefetchScalarGridSpec(
            num_scalar_prefetch=2,
            grid=(_HEADS, n_pairs),
            in_specs=[
                vec(_HEAD_DIM), vec(_HEAD_DIM), vec(_HEAD_DIM), vec(_HEAD_DIM),
                pl.BlockSpec((_V_DIM, 1), lambda h, p, qt, kt: (0, 0)),
                pl.BlockSpec((tq, _V_DIM), lambda h, p, qt, kt: (qt[p], h)),
                pl.BlockSpec((tk, _V_DIM), lambda h, p, qt, kt: (kt[p], k_blk0 + h)),
                pl.BlockSpec((_V_DIM, tk), lambda h, p, qt, kt: (h, kt[p])),
                *cast_in_specs,
            ],
            out_specs=[pl.BlockSpec((tq, _V_DIM), lambda h, p, qt, kt: (qt[p], h)),
                       *cast_out_specs],
            scratch_shapes=[
                pltpu.VMEM((2, 1, tq), _F32),
                pltpu.VMEM((2, 1, tq), _F32),
                pltpu.VMEM((2, _V_DIM, tq), _F32),
            ],
        ),
        out_shape=[jax.ShapeDtypeStruct((s_len, _HEADS * _V_DIM), _BF16),
                   *[jax.ShapeDtypeStruct((w.shape[0], hi - lo), _BF16)
                     for w, lo, hi in cast_weights]],
        compiler_params=_params(("arbitrary", "arbitrary")),
        name="diff_attn",
    )(jnp.asarray(q_blocks), jnp.asarray(k_blocks), lq1, lk1, lq2, lk2, subw_col, qk, qk, vt,
      *cast_arrays)
    return outs[0], outs[1:]


def _sgu_kernel(u_ref, s_ref, g_ref, b_ref, w_ref, bs_ref, o_ref, *, tm):
    gd = _SEG // _SGU_GROUPS
    sn = _layer_norm_rows(s_ref[...].astype(_F32), g_ref[...], b_ref[...]).astype(_BF16)
    row = lax.broadcasted_iota(jnp.int32, (_CHUNK, _CHUNK), 0)
    col = lax.broadcasted_iota(jnp.int32, (_CHUNK, _CHUNK), 1)
    for g in range(_SGU_GROUPS):
        w = jnp.where(row >= col, w_ref[g], 0.0).astype(_BF16)
        bias = bs_ref[:, g:g + 1]
        cs = slice(g * gd, (g + 1) * gd)
        for c in range(tm // _CHUNK):
            rs = slice(c * _CHUNK, (c + 1) * _CHUNK)
            mixed = _dot(w, sn[rs, cs]) + bias
            o_ref[rs, cs] = (u_ref[rs, cs].astype(_F32) * mixed).astype(o_ref.dtype)


def _sgu(us, ln_g, ln_b, w_s, b_s_t, tm=1024):
    s_len = us.shape[0]
    return pl.pallas_call(
        functools.partial(_sgu_kernel, tm=tm),
        grid=(s_len // tm,),
        in_specs=[
            pl.BlockSpec((tm, _SEG), lambda i: (i, 0)),
            pl.BlockSpec((tm, _SEG), lambda i: (i, 1)),
            pl.BlockSpec((1, _SEG), lambda i: (0, 0)),
            pl.BlockSpec((1, _SEG), lambda i: (0, 0)),
            pl.BlockSpec((_SGU_GROUPS, _CHUNK, _CHUNK), lambda i: (0, 0, 0)),
            pl.BlockSpec((_CHUNK, _SGU_GROUPS), lambda i: (0, 0)),
        ],
        out_specs=pl.BlockSpec((tm, _SEG), lambda i: (i, 0)),
        out_shape=jax.ShapeDtypeStruct((s_len, _SEG), _BF16),
        compiler_params=_params(("arbitrary",)),
        name="sgu",
    )(us, us, ln_g, ln_b, w_s, b_s_t)


def _merge_kernel(ya_ref, yb_ref, wa_ref, wb_ref, ga_ref, gb_ref, o_ref):
    wa = wa_ref[...].astype(_BF16)
    wb = wb_ref[...].astype(_BF16)
    slab = ya_ref.shape[0] // _ROW_SPLITS
    for s in range(_ROW_SPLITS):
        rows = slice(s * slab, (s + 1) * slab)
        a = _dot(ya_ref[rows, :], wa)
        b = _dot(yb_ref[rows, :], wb)
        o_ref[rows, :] = (ga_ref[rows, :].astype(_F32) * a
                          + gb_ref[rows, :].astype(_F32) * b).astype(o_ref.dtype)


def _merge(y_a, y_b, w_a, w_b, gates, tm=_TM, tn=_TN_BF16):
    m, k = y_a.shape
    n = w_a.shape[1]
    return pl.pallas_call(
        _merge_kernel,
        grid=(m // tm, n // tn),
        in_specs=[
            pl.BlockSpec((tm, k), lambda i, j: (i, 0)),
            pl.BlockSpec((tm, k), lambda i, j: (i, 0)),
            pl.BlockSpec((k, tn), lambda i, j: (0, j)),
            pl.BlockSpec((k, tn), lambda i, j: (0, j)),
            _tile_spec(tm, tn),
            _tile_spec(tm, tn, n // tn),
        ],
        out_specs=pl.BlockSpec((tm, tn), lambda i, j: (i, j)),
        out_shape=jax.ShapeDtypeStruct((m, n), _BF16),
        compiler_params=_params(("arbitrary", "arbitrary")),
        name="merge",
    )(y_a, y_b, w_a, w_b, gates, gates)


def _ln_bf16_kernel(x_ref, g_ref, b_ref, o_ref):
    o_ref[...] = _layer_norm_rows(x_ref[...], g_ref[...], b_ref[...]).astype(o_ref.dtype)


def _ln_bf16(x, g, b, tm=512):
    m, n = x.shape
    row = pl.BlockSpec((tm, n), lambda i: (i, 0))
    vec = pl.BlockSpec((1, n), lambda i: (0, 0))
    return pl.pallas_call(
        _ln_bf16_kernel,
        grid=(m // tm,),
        in_specs=[row, vec, vec],
        out_specs=row,
        out_shape=jax.ShapeDtypeStruct((m, n), _BF16),
        compiler_params=_params(("arbitrary",)),
        name="ln1",
    )(x, g, b)


def _final_kernel(pre_ref, ff_ref, g1_ref, b1_ref, bo_ref, g2_ref, b2_ref, o_ref):
    x = pre_ref[...]
    xc = x - jnp.mean(x, axis=-1, keepdims=True)
    var = jnp.mean(xc * xc, axis=-1, keepdims=True)
    shift = _ALPHA * b1_ref[...] + bo_ref[...]
    y = xc * (_ALPHA * lax.rsqrt(var + _LN_EPS)) * g1_ref[...] + (ff_ref[...] + shift)
    o_ref[...] = _layer_norm_rows(y, g2_ref[...], b2_ref[...])


def _final(pre1, ff, g1, b1, b_out, g2, b2, tm=256):
    m, n = pre1.shape
    row = pl.BlockSpec((tm, n), lambda i: (i, 0))
    vec = pl.BlockSpec((1, n), lambda i: (0, 0))
    return pl.pallas_call(
        _final_kernel,
        grid=(m // tm,),
        in_specs=[row, row, vec, vec, vec, vec, vec],
        out_specs=row,
        out_shape=jax.ShapeDtypeStruct((m, n), _F32),
        compiler_params=_params(("arbitrary",)),
        name="final_ln",
    )(pre1, ff, g1, b1, b_out, g2, b2)


def _mlp_out_kernel(z_ref, w_ref, o_ref):
    kk = pl.program_id(2)
    slab = z_ref.shape[0] // _ROW_SPLITS

    def accumulate(first):
        wb = w_ref[...].astype(_BF16)
        for s in range(_ROW_SPLITS):
            rows = slice(s * slab, (s + 1) * slab)
            part = _dot(z_ref[rows, :], wb)
            o_ref[rows, :] = part if first else o_ref[rows, :] + part

    @pl.when(kk == 0)
    def _():
        accumulate(True)

    @pl.when(kk > 0)
    def _():
        accumulate(False)


def _mlp_out(z, w, tm=2048, tn=1024, tk=2048):
    m, k = z.shape
    n = w.shape[1]
    return pl.pallas_call(
        _mlp_out_kernel,
        grid=(m // tm, n // tn, k // tk),
        in_specs=[
            pl.BlockSpec((tm, tk), lambda i, j, kk: (i, kk)),
            pl.BlockSpec((tk, tn), lambda i, j, kk: (kk, j)),
        ],
        out_specs=pl.BlockSpec((tm, tn), lambda i, j, kk: (i, j)),
        out_shape=jax.ShapeDtypeStruct((m, n), _F32),
        compiler_params=_params(("arbitrary", "arbitrary", "arbitrary")),
        name="mlp_out",
    )(z, w)


def _rope_tables(s_len):
    inv_freq = _ROPE_THETA ** (-np.arange(0, _ROPE_DIM, 2, dtype=np.float64) / _ROPE_DIM)
    ang = np.arange(s_len, dtype=np.float64)[:, None] * inv_freq[None, :]
    cos, sin = np.cos(ang), np.sin(ang)
    ones = np.ones((s_len, _LANES - _ROPE_DIM))
    zeros_h = np.zeros((s_len, _ROPE_HALF))
    zeros_r = np.zeros((s_len, _LANES - _ROPE_DIM))
    cos_t = np.concatenate([cos, cos, ones], axis=1)
    sin_up = np.concatenate([zeros_h, sin, zeros_r], axis=1)
    sin_dn = np.concatenate([-sin, zeros_h, zeros_r], axis=1)
    return tuple(jnp.asarray(t, dtype=_F32) for t in (cos_t, sin_up, sin_dn))


def kernel(x, w_in, lambda_q1, lambda_k1, lambda_q2, lambda_k2, subln_w, sgu_ln_g, sgu_ln_b,
           w_spatial, b_spatial, w_proj_attn, w_proj_sgu, w_out, ln1_g, ln1_b, w_mlp_in, b_mlp_in,
           w_mlp_out, b_mlp_out, ln2_g, ln2_b):
    bsz, s_len, d = x.shape
    assert bsz == 1 and w_in.shape[0] == _DEPTH == 1
    h = x.reshape(s_len, d)
    l = 0
    lambda_init = 0.8 - 0.6 * math.exp(-0.3 * l)
    cos_t, sin_up, sin_dn = _rope_tables(s_len)
    q_scale = _HEAD_DIM ** -0.5 * math.log2(math.e)
    qk_scale = jnp.concatenate([jnp.full((1, _SEG), q_scale, _F32), jnp.ones((1, _SEG), _F32)], axis=1)

    hb = _cast_bf16(h)
    w_in_l = w_in[l]
    qk = _linear("proj_qk", hb, w_in_l, 0, 2 * _SEG, _rope_epilogue,
                 extras=[(cos_t, _row_spec(_TM, _LANES)), (sin_up, _row_spec(_TM, _LANES)),
                         (sin_dn, _row_spec(_TM, _LANES)), (qk_scale, _col_spec(_TN))],
                 row_splits=2 * _ROW_SPLITS)
    vt = _linear("proj_v", hb, w_in_l, 2 * _SEG, _SEG, _transpose_epilogue, transposed_out=True)

    y_a, (w_tail,) = _attention(
        qk, vt, lambda_q1[l][None], lambda_k1[l][None], lambda_q2[l][None], lambda_k2[l][None],
        subln_w[l][:, None], lambda_init, cast_weights=[(w_in_l, 3 * _SEG, w_in_l.shape[1])])
    us, (w_pa, w_ps, w_o) = _linear("proj_us", hb, w_tail, 0, 2 * _SEG, _gelu_epilogue,
                                    tn=_TN_BF16,
                                    convert=[w_proj_attn[l], w_proj_sgu[l], w_out[l]])
    gates, (w_mi,) = _linear("proj_gates", hb, w_tail, 2 * _SEG, 2 * d, _sigmoid_epilogue,
                             tn=_TN_BF16, row_splits=_ROW_SPLITS, convert=[w_mlp_in[l]])
    y_b = _sgu(us, sgu_ln_g[l][None], sgu_ln_b[l][None], w_spatial[l], b_spatial[l].T)
    merged = _merge(y_a, y_b, w_pa, w_ps, gates)
    pre1 = _linear("outproj", merged, w_o, 0, d, _residual_epilogue,
                   extras=[(h, _tile_spec(_TM, _TN_BF16))], out_dtype=_F32, tn=_TN_BF16)
    h1b = _ln_bf16(pre1, ln1_g[l][None], ln1_b[l][None])
    z, (w_mo,) = _linear("mlp_in", h1b, w_mi, 0, w_mi.shape[1], _relu2_epilogue,
                         extras=[(b_mlp_in[l][None], _col_spec(_TN_BF16))], tn=_TN_BF16,
                         convert=[w_mlp_out[l]])
    ff = _mlp_out(z, w_mo)
    out = _final(pre1, ff, ln1_g[l][None], ln1_b[l][None], b_mlp_out[l][None],
                 ln2_g[l][None], ln2_b[l][None])
    return out.reshape(bsz, s_len, d)
```

```python
import functools
import math

import numpy as np
import jax
import jax.numpy as jnp
from jax import lax
from jax.experimental import pallas as pl
from jax.experimental.pallas import tpu as pltpu

_F32 = jnp.float32
_BF16 = jnp.bfloat16

_LANES = 128
_VMEM_LIMIT_BYTES = 60 * 1024 * 1024

_HEADS = 8
_HEAD_DIM = 128
_V_DIM = 2 * _HEAD_DIM
_ROPE_DIM = _HEAD_DIM // 4
_ROPE_HALF = _ROPE_DIM // 2
_ROPE_THETA = 500000.0
_SGU_GROUPS = 8
_CHUNK = 128
_DEPTH = 1
_ALPHA = (2.0 * _DEPTH) ** 0.25
_LN_EPS = 1e-5
_SEG = 2048

_TM = 1024
_TN = 512
_TN_BF16 = 1024
_ROW_SPLITS = 4
_CAST_COL_BLOCKS = 8
_CAST_PIECES = 8


def _params(sem):
    return pltpu.CompilerParams(dimension_semantics=sem, vmem_limit_bytes=_VMEM_LIMIT_BYTES)


def _dot(a, b):
    return jnp.dot(a, b, preferred_element_type=_F32)


def _layer_norm_rows(x, g, b):
    mu = jnp.mean(x, axis=-1, keepdims=True)
    xc = x - mu
    var = jnp.mean(xc * xc, axis=-1, keepdims=True)
    return xc * lax.rsqrt(var + _LN_EPS) * g + b


def _proj_v_kernel(x_ref, w_ref, xb_ref, vt_ref):
    @pl.when(pl.program_id(1) == 0)
    def _():
        xb_ref[...] = x_ref[...].astype(xb_ref.dtype)

    acc = _dot(xb_ref[...], w_ref[...].astype(_BF16))
    vt_ref[...] = acc.T.astype(vt_ref.dtype)


def _proj_v(x, w, col0, n_cols, tm=512, tn=_TN):
    m, k = x.shape
    blk0 = col0 // tn
    return pl.pallas_call(
        _proj_v_kernel,
        grid=(m // tm, n_cols // tn),
        in_specs=[
            pl.BlockSpec((tm, k), lambda i, j: (i, 0)),
            pl.BlockSpec((k, tn), lambda i, j: (0, blk0 + j)),
        ],
        out_specs=[
            pl.BlockSpec((tm, k), lambda i, j: (i, 0)),
            pl.BlockSpec((tn, tm), lambda i, j: (j, i)),
        ],
        out_shape=[jax.ShapeDtypeStruct((m, k), _BF16), jax.ShapeDtypeStruct((n_cols, m), _BF16)],
        compiler_params=_params(("arbitrary", "arbitrary")),
        name="proj_v",
    )(x, w)


def _linear_kernel(a_ref, w_ref, *rest, epilogue, row_splits, n_extra, n_cast):
    extras = rest[:n_extra]
    cast_in = rest[n_extra:n_extra + n_cast]
    o_ref = rest[n_extra + n_cast]
    cast_out = rest[n_extra + n_cast + 1:]
    wb = w_ref[...].astype(_BF16)
    slab = a_ref.shape[0] // row_splits
    for s in range(row_splits):
        rows = slice(s * slab, (s + 1) * slab)
        epilogue(_dot(a_ref[rows, :], wb), rows, o_ref, *extras)
    for src, dst in zip(cast_in, cast_out):
        dst[...] = src[...].astype(dst.dtype)


def _row_spec(tm, width):
    return pl.BlockSpec((tm, width), lambda i, j: (i, 0))


def _col_spec(tn, blk0=0):
    return pl.BlockSpec((1, tn), lambda i, j: (0, blk0 + j))


def _tile_spec(tm, tn, blk0=0):
    return pl.BlockSpec((tm, tn), lambda i, j: (i, blk0 + j))


def _linear(name, a, w, col0, n_cols, epilogue, extras=(), out_dtype=_BF16, tm=_TM, tn=_TN,
            row_splits=1, convert=()):
    m, k = a.shape
    blk0 = col0 // tn
    n_i, n_j = m // tm, n_cols // tn
    arrays = [x for x, _ in extras]
    specs = [s for _, s in extras]
    out_spec = pl.BlockSpec((tm, tn), lambda i, j: (i, j))
    out_shape = jax.ShapeDtypeStruct((m, n_cols), out_dtype)
    band_specs = [pl.BlockSpec((c.shape[0] // (n_i * n_j), c.shape[1]),
                               lambda i, j: (i * n_j + j, 0)) for c in convert]
    outs = pl.pallas_call(
        functools.partial(_linear_kernel, epilogue=epilogue, row_splits=row_splits,
                          n_extra=len(extras), n_cast=len(convert)),
        grid=(n_i, n_j),
        in_specs=[
            pl.BlockSpec((tm, k), lambda i, j: (i, 0)),
            pl.BlockSpec((k, tn), lambda i, j: (0, blk0 + j)),
            *specs,
            *band_specs,
        ],
        out_specs=[out_spec, *band_specs],
        out_shape=[out_shape, *[jax.ShapeDtypeStruct(c.shape, _BF16) for c in convert]],
        compiler_params=_params(("arbitrary", "arbitrary")),
        name=name,
    )(a, w, *arrays, *convert)
    return (outs[0], outs[1:]) if convert else outs[0]


def _gelu_epilogue(acc, rows, o_ref):
    o_ref[rows, :] = jax.nn.gelu(acc).astype(o_ref.dtype)


def _sigmoid_epilogue(acc, rows, o_ref):
    o_ref[rows, :] = jax.nn.sigmoid(acc).astype(o_ref.dtype)


def _rope_epilogue(acc, rows, o_ref, cos_ref, sup_ref, sdn_ref, scale_ref):
    cos, sup, sdn = cos_ref[rows, :], sup_ref[rows, :], sdn_ref[rows, :]
    for c in range(acc.shape[1] // _LANES):
        sl = slice(c * _LANES, (c + 1) * _LANES)
        t = acc[:, sl]
        r = (t * cos
             + pltpu.roll(t, _ROPE_HALF, 1) * sup
             + pltpu.roll(t, _LANES - _ROPE_HALF, 1) * sdn)
        o_ref[rows, sl] = (r * scale_ref[:, sl]).astype(o_ref.dtype)


def _residual_epilogue(acc, rows, o_ref, x_ref):
    o_ref[rows, :] = _ALPHA * x_ref[rows, :] + acc


def _relu2_epilogue(acc, rows, o_ref, b_ref):
    z = jnp.maximum(acc + b_ref[...], 0.0)
    o_ref[rows, :] = (z * z).astype(o_ref.dtype)


def _attn_kernel(qi_tab, ki_tab, lq1_ref, lk1_ref, lq2_ref, lk2_ref, subw_ref,
                 q_ref, k_ref, vt_ref, *rest, n_cast, tq, tk, rq, ks, kh, ahead, lambda_init):
    cast_in = rest[:n_cast]
    o_ref = rest[n_cast]
    cast_out = rest[n_cast + 1:2 * n_cast + 1]
    m_ref, l_ref, acc_ref = rest[2 * n_cast + 1:]
    p_idx = pl.program_id(1)
    qi = qi_tab[p_idx]
    ki = ki_tab[p_idx]

    @pl.when(ki == 0)
    def _():
        m_ref[...] = jnp.full(m_ref.shape, -jnp.inf, _F32)
        l_ref[...] = jnp.zeros(l_ref.shape, _F32)
        acc_ref[...] = jnp.zeros(acc_ref.shape, _F32)

    def step(off):

        def span_off(b):
            return None if off is None else off + b * ks

        def keys_used(r, b):
            if off is None:
                return ks
            last_q = (r + 1) * rq - 1
            return kh * sum(1 for h in range(ks // kh) if span_off(b) + h * kh <= last_q)

        def needs_mask(r, b):
            return off is not None and span_off(b) + keys_used(r, b) - 1 > r * rq

        chunks = [(c, r, b) for b in range(tk // ks) for r in range(tq // rq) for c in range(2)
                  if keys_used(r, b) > 0]
        is_last_block = off is not None and off + tk >= tq
        last_chunk_of = {r: i for i, (_, r, _) in enumerate(chunks)}

        def scores(c, r, b):
            sl = slice(c * _HEAD_DIM, (c + 1) * _HEAD_DIM)
            q = q_ref[r * rq:(r + 1) * rq, sl]
            parts = [lax.dot_general(k_ref[b * ks + h * kh:b * ks + (h + 1) * kh, sl], q,
                                     (((1,), (1,)), ((), ())), preferred_element_type=_F32)
                     for h in range(keys_used(r, b) // kh)]
            return jnp.concatenate(parts, axis=0)

        def softmax_pv(c, r, b, st):
            qs = slice(r * rq, (r + 1) * rq)
            nk = keys_used(r, b)
            if needs_mask(r, b):
                delta = (lax.broadcasted_iota(jnp.int32, (nk, rq), 0)
                         - lax.broadcasted_iota(jnp.int32, (nk, rq), 1))
                st = jnp.where(delta <= r * rq - span_off(b), st, -jnp.inf)
            m_prev = m_ref[c, :, qs]
            m_new = jnp.maximum(m_prev, jnp.max(st, axis=0, keepdims=True))
            alpha = jnp.exp2(m_prev - m_new)
            p = jnp.exp2(st - m_new)
            l_ref[c, :, qs] = alpha * l_ref[c, :, qs] + jnp.sum(p, axis=0, keepdims=True)
            acc_ref[c, :, qs] = (alpha * acc_ref[c, :, qs]
                                 + _dot(vt_ref[:, b * ks:b * ks + nk], p.astype(_BF16)))
            m_ref[c, :, qs] = m_new

        pending = [scores(*ch) for ch in chunks[:ahead]]
        def cast_piece(i):
            for src, dst in zip(cast_in, cast_out):
                rows = src.shape[0] // _CAST_PIECES
                dst[i * rows:(i + 1) * rows, :] = src[i * rows:(i + 1) * rows, :].astype(dst.dtype)

        assert len(chunks) >= _CAST_PIECES
        if is_last_block:
            for r in range(tq // rq):
                if r not in last_chunk_of:
                    write_out(r)
        for i, ch in enumerate(chunks):
            if i + ahead < len(chunks):
                pending.append(scores(*chunks[i + ahead]))
            softmax_pv(*ch, pending.pop(0))
            if i < _CAST_PIECES:
                cast_piece(i)
            if is_last_block and last_chunk_of[ch[1]] == i:
                write_out(ch[1])

    def write_out(r):
        qs = slice(r * rq, (r + 1) * rq)
        lam = (jnp.exp(jnp.sum(lq1_ref[...] * lk1_ref[...], axis=1, keepdims=True))
               - jnp.exp(jnp.sum(lq2_ref[...] * lk2_ref[...], axis=1, keepdims=True))
               + lambda_init)
        ot = (acc_ref[0, :, qs] / l_ref[0, :, qs]
              - lam * (acc_ref[1, :, qs] / l_ref[1, :, qs]))
        ot = ot * lax.rsqrt(jnp.mean(ot * ot, axis=0, keepdims=True) + _LN_EPS)
        ot = ot * subw_ref[...] * (1.0 - lambda_init)
        o_ref[qs, :] = ot.T.astype(o_ref.dtype)

    off = ki * tk - qi * tq

    @pl.when(off + tk <= 0)
    def _():
        step(None)

    for diag_off in range(0, tq, tk):
        @pl.when(off == diag_off)
        def _(diag_off=diag_off):
            step(diag_off)


def _causal_pairs(n_q, tq, tk):
    qs, ks = [], []
    for qi in range(n_q):
        for ki in range(((qi + 1) * tq - 1) // tk + 1):
            qs.append(qi)
            ks.append(ki)
    return np.asarray(qs, np.int32), np.asarray(ks, np.int32)


def _attention(qk, vt, lq1, lk1, lq2, lk2, subw_col, lambda_init, cast_weights,
               tq=2048, tk=2048, rq=512, ks=1024, kh=512, ahead=2):
    s_len = qk.shape[0]
    q_blocks, k_blocks = _causal_pairs(s_len // tq, tq, tk)
    n_pairs = len(q_blocks)
    assert n_pairs >= _CAST_COL_BLOCKS
    k_blk0 = _SEG // _V_DIM
    vec = lambda n: pl.BlockSpec((1, n), lambda h, p, qt, kt: (0, 0))

    def cast_specs_of(w, col_lo, col_hi):
        rows, cols = w.shape[0] // _HEADS, (col_hi - col_lo) // _CAST_COL_BLOCKS
        blk0 = col_lo // cols
        band = lambda p: jnp.minimum(p, _CAST_COL_BLOCKS - 1)
        return (pl.BlockSpec((rows, cols), lambda h, p, qt, kt: (h, blk0 + band(p))),
                pl.BlockSpec((rows, cols), lambda h, p, qt, kt: (h, band(p))))

    cast_in_specs, cast_out_specs = zip(*[cast_specs_of(*cw) for cw in cast_weights])
    cast_arrays = [w for w, _, _ in cast_weights]
    kern = functools.partial(_attn_kernel, n_cast=len(cast_weights), tq=tq, tk=tk, rq=rq, ks=ks,
                             kh=kh, ahead=ahead, lambda_init=lambda_init)
    outs = pl.pallas_call(
        kern,
        grid_spec=pltpu.PrefetchScalarGridSpec(
            num_scalar_prefetch=2,
            grid=(_HEADS, n_pairs),
            in_specs=[
                vec(_HEAD_DIM), vec(_HEAD_DIM), vec(_HEAD_DIM), vec(_HEAD_DIM),
                pl.BlockSpec((_V_DIM, 1), lambda h, p, qt, kt: (0, 0)),
                pl.BlockSpec((tq, _V_DIM), lambda h, p, qt, kt: (qt[p], h)),
                pl.BlockSpec((tk, _V_DIM), lambda h, p, qt, kt: (kt[p], k_blk0 + h)),
                pl.BlockSpec((_V_DIM, tk), lambda h, p, qt, kt: (h, kt[p])),
                *cast_in_specs,
            ],
            out_specs=[pl.BlockSpec((tq, _V_DIM), lambda h, p, qt, kt: (qt[p], h)),
                       *cast_out_specs],
            scratch_shapes=[
                pltpu.VMEM((2, 1, tq), _F32),
                pltpu.VMEM((2, 1, tq), _F32),
                pltpu.VMEM((2, _V_DIM, tq), _F32),
            ],
        ),
        out_shape=[jax.ShapeDtypeStruct((s_len, _HEADS * _V_DIM), _BF16),
                   *[jax.ShapeDtypeStruct((w.shape[0], hi - lo), _BF16)
                     for w, lo, hi in cast_weights]],
        compiler_params=_params(("arbitrary", "arbitrary")),
        name="diff_attn",
    )(jnp.asarray(q_blocks), jnp.asarray(k_blocks), lq1, lk1, lq2, lk2, subw_col, qk, qk, vt,
      *cast_arrays)
    return outs[0], outs[1:]


def _sgu_kernel(u_ref, s_ref, g_ref, b_ref, w_ref, bs_ref, o_ref, *, tm):
    gd = _SEG // _SGU_GROUPS
    sn = _layer_norm_rows(s_ref[...].astype(_F32), g_ref[...], b_ref[...]).astype(_BF16)
    row = lax.broadcasted_iota(jnp.int32, (_CHUNK, _CHUNK), 0)
    col = lax.broadcasted_iota(jnp.int32, (_CHUNK, _CHUNK), 1)
    for g in range(_SGU_GROUPS):
        w = jnp.where(row >= col, w_ref[g], 0.0).astype(_BF16)
        bias = bs_ref[:, g:g + 1]
        cs = slice(g * gd, (g + 1) * gd)
        for c in range(tm // _CHUNK):
            rs = slice(c * _CHUNK, (c + 1) * _CHUNK)
            mixed = _dot(w, sn[rs, cs]) + bias
            o_ref[rs, cs] = (u_ref[rs, cs].astype(_F32) * mixed).astype(o_ref.dtype)


def _sgu(us, ln_g, ln_b, w_s, b_s_t, tm=1024):
    s_len = us.shape[0]
    return pl.pallas_call(
        functools.partial(_sgu_kernel, tm=tm),
        grid=(s_len // tm,),
        in_specs=[
            pl.BlockSpec((tm, _SEG), lambda i: (i, 0)),
            pl.BlockSpec((tm, _SEG), lambda i: (i, 1)),
            pl.BlockSpec((1, _SEG), lambda i: (0, 0)),
            pl.BlockSpec((1, _SEG), lambda i: (0, 0)),
            pl.BlockSpec((_SGU_GROUPS, _CHUNK, _CHUNK), lambda i: (0, 0, 0)),
            pl.BlockSpec((_CHUNK, _SGU_GROUPS), lambda i: (0, 0)),
        ],
        out_specs=pl.BlockSpec((tm, _SEG), lambda i: (i, 0)),
        out_shape=jax.ShapeDtypeStruct((s_len, _SEG), _BF16),
        compiler_params=_params(("arbitrary",)),
        name="sgu",
    )(us, us, ln_g, ln_b, w_s, b_s_t)


def _merge_kernel(ya_ref, yb_ref, wa_ref, wb_ref, ga_ref, gb_ref, o_ref):
    wa = wa_ref[...].astype(_BF16)
    wb = wb_ref[...].astype(_BF16)
    slab = ya_ref.shape[0] // _ROW_SPLITS
    for s in range(_ROW_SPLITS):
        rows = slice(s * slab, (s + 1) * slab)
        a = _dot(ya_ref[rows, :], wa)
        b = _dot(yb_ref[rows, :], wb)
        o_ref[rows, :] = (ga_ref[rows, :].astype(_F32) * a
                          + gb_ref[rows, :].astype(_F32) * b).astype(o_ref.dtype)


def _merge(y_a, y_b, w_a, w_b, gates, tm=_TM, tn=_TN_BF16):
    m, k = y_a.shape
    n = w_a.shape[1]
    return pl.pallas_call(
        _merge_kernel,
        grid=(m // tm, n // tn),
        in_specs=[
            pl.BlockSpec((tm, k), lambda i, j: (i, 0)),
            pl.BlockSpec((tm, k), lambda i, j: (i, 0)),
            pl.BlockSpec((k, tn), lambda i, j: (0, j)),
            pl.BlockSpec((k, tn), lambda i, j: (0, j)),
            _tile_spec(tm, tn),
            _tile_spec(tm, tn, n // tn),
        ],
        out_specs=pl.BlockSpec((tm, tn), lambda i, j: (i, j)),
        out_shape=jax.ShapeDtypeStruct((m, n), _BF16),
        compiler_params=_params(("arbitrary", "arbitrary")),
        name="merge",
    )(y_a, y_b, w_a, w_b, gates, gates)


def _ln_bf16_kernel(x_ref, g_ref, b_ref, o_ref):
    o_ref[...] = _layer_norm_rows(x_ref[...], g_ref[...], b_ref[...]).astype(o_ref.dtype)


def _ln_bf16(x, g, b, tm=512):
    m, n = x.shape
    row = pl.BlockSpec((tm, n), lambda i: (i, 0))
    vec = pl.BlockSpec((1, n), lambda i: (0, 0))
    return pl.pallas_call(
        _ln_bf16_kernel,
        grid=(m // tm,),
        in_specs=[row, vec, vec],
        out_specs=row,
        out_shape=jax.ShapeDtypeStruct((m, n), _BF16),
        compiler_params=_params(("arbitrary",)),
        name="ln1",
    )(x, g, b)


def _final_kernel(pre_ref, ff_ref, g1_ref, b1_ref, bo_ref, g2_ref, b2_ref, o_ref):
    x = pre_ref[...]
    xc = x - jnp.mean(x, axis=-1, keepdims=True)
    var = jnp.mean(xc * xc, axis=-1, keepdims=True)
    shift = _ALPHA * b1_ref[...] + bo_ref[...]
    y = xc * (_ALPHA * lax.rsqrt(var + _LN_EPS)) * g1_ref[...] + (ff_ref[...] + shift)
    o_ref[...] = _layer_norm_rows(y, g2_ref[...], b2_ref[...])


def _final(pre1, ff, g1, b1, b_out, g2, b2, tm=256):
    m, n = pre1.shape
    row = pl.BlockSpec((tm, n), lambda i: (i, 0))
    vec = pl.BlockSpec((1, n), lambda i: (0, 0))
    return pl.pallas_call(
        _final_kernel,
        grid=(m // tm,),
        in_specs=[row, row, vec, vec, vec, vec, vec],
        out_specs=row,
        out_shape=jax.ShapeDtypeStruct((m, n), _F32),
        compiler_params=_params(("arbitrary",)),
        name="final_ln",
    )(pre1, ff, g1, b1, b_out, g2, b2)


def _mlp_out_kernel(z_ref, w_ref, o_ref):
    kk = pl.program_id(2)
    slab = z_ref.shape[0] // _ROW_SPLITS

    def accumulate(first):
        wb = w_ref[...].astype(_BF16)
        for s in range(_ROW_SPLITS):
            rows = slice(s * slab, (s + 1) * slab)
            part = _dot(z_ref[rows, :], wb)
            o_ref[rows, :] = part if first else o_ref[rows, :] + part

    @pl.when(kk == 0)
    def _():
        accumulate(True)

    @pl.when(kk > 0)
    def _():
        accumulate(False)


def _mlp_out(z, w, tm=2048, tn=1024, tk=2048):
    m, k = z.shape
    n = w.shape[1]
    return pl.pallas_call(
        _mlp_out_kernel,
        grid=(m // tm, n // tn, k // tk),
        in_specs=[
            pl.BlockSpec((tm, tk), lambda i, j, kk: (i, kk)),
            pl.BlockSpec((tk, tn), lambda i, j, kk: (kk, j)),
        ],
        out_specs=pl.BlockSpec((tm, tn), lambda i, j, kk: (i, j)),
        out_shape=jax.ShapeDtypeStruct((m, n), _F32),
        compiler_params=_params(("arbitrary", "arbitrary", "arbitrary")),
        name="mlp_out",
    )(z, w)


def _rope_tables(s_len):
    inv_freq = _ROPE_THETA ** (-np.arange(0, _ROPE_DIM, 2, dtype=np.float64) / _ROPE_DIM)
    ang = np.arange(s_len, dtype=np.float64)[:, None] * inv_freq[None, :]
    cos, sin = np.cos(ang), np.sin(ang)
    ones = np.ones((s_len, _LANES - _ROPE_DIM))
    zeros_h = np.zeros((s_len, _ROPE_HALF))
    zeros_r = np.zeros((s_len, _LANES - _ROPE_DIM))
    cos_t = np.concatenate([cos, cos, ones], axis=1)
    sin_up = np.concatenate([zeros_h, sin, zeros_r], axis=1)
    sin_dn = np.concatenate([-sin, zeros_h, zeros_r], axis=1)
    return tuple(jnp.asarray(t, dtype=_F32) for t in (cos_t, sin_up, sin_dn))


def kernel(x, w_in, lambda_q1, lambda_k1, lambda_q2, lambda_k2, subln_w, sgu_ln_g, sgu_ln_b,
           w_spatial, b_spatial, w_proj_attn, w_proj_sgu, w_out, ln1_g, ln1_b, w_mlp_in, b_mlp_in,
           w_mlp_out, b_mlp_out, ln2_g, ln2_b):
    bsz, s_len, d = x.shape
    assert bsz == 1 and w_in.shape[0] == _DEPTH == 1
    h = x.reshape(s_len, d)
    l = 0
    lambda_init = 0.8 - 0.6 * math.exp(-0.3 * l)
    cos_t, sin_up, sin_dn = _rope_tables(s_len)
    q_scale = _HEAD_DIM ** -0.5 * math.log2(math.e)
    qk_scale = jnp.concatenate([jnp.full((1, _SEG), q_scale, _F32), jnp.ones((1, _SEG), _F32)], axis=1)

    w_in_l = w_in[l]
    hb, vt = _proj_v(h, w_in_l, 2 * _SEG, _SEG)
    qk = _linear("proj_qk", hb, w_in_l, 0, 2 * _SEG, _rope_epilogue,
                 extras=[(cos_t, _row_spec(_TM, _LANES)), (sin_up, _row_spec(_TM, _LANES)),
                         (sin_dn, _row_spec(_TM, _LANES)), (qk_scale, _col_spec(_TN))],
                 row_splits=2 * _ROW_SPLITS)

    y_a, (w_tail,) = _attention(
        qk, vt, lambda_q1[l][None], lambda_k1[l][None], lambda_q2[l][None], lambda_k2[l][None],
        subln_w[l][:, None], lambda_init, cast_weights=[(w_in_l, 3 * _SEG, w_in_l.shape[1])])
    us, (w_pa, w_ps, w_o) = _linear("proj_us", hb, w_tail, 0, 2 * _SEG, _gelu_epilogue,
                                    tn=_TN_BF16,
                                    convert=[w_proj_attn[l], w_proj_sgu[l], w_out[l]])
    gates, (w_mi,) = _linear("proj_gates", hb, w_tail, 2 * _SEG, 2 * d, _sigmoid_epilogue,
                             tn=_TN_BF16, row_splits=_ROW_SPLITS, convert=[w_mlp_in[l]])
    y_b = _sgu(us, sgu_ln_g[l][None], sgu_ln_b[l][None], w_spatial[l], b_spatial[l].T)
    merged = _merge(y_a, y_b, w_pa, w_ps, gates)
    pre1 = _linear("outproj", merged, w_o, 0, d, _residual_epilogue,
                   extras=[(h, _tile_spec(_TM, _TN_BF16))], out_dtype=_F32, tn=_TN_BF16)
    h1b = _ln_bf16(pre1, ln1_g[l][None], ln1_b[l][None])
    z, (w_mo,) = _linear("mlp_in", h1b, w_mi, 0, w_mi.shape[1], _relu2_epilogue,
                         extras=[(b_mlp_in[l][None], _col_spec(_TN_BF16))], tn=_TN_BF16,
                         convert=[w_mlp_out[l]])
    ff = _mlp_out(z, w_mo)
    out = _final(pre1, ff, ln1_g[l][None], ln1_b[l][None], b_mlp_out[l][None],
                 ln2_g[l][None], ln2_b[l][None])
    return out.reshape(bsz, s_len, d)
```

```python
import functools
import math

import numpy as np
import jax
import jax.numpy as jnp
from jax import lax
from jax.experimental import pallas as pl
from jax.experimental.pallas import tpu as pltpu

_F32 = jnp.float32
_BF16 = jnp.bfloat16

_LANES = 128
_VMEM_LIMIT_BYTES = 60 * 1024 * 1024

_HEADS = 8
_HEAD_DIM = 128
_V_DIM = 2 * _HEAD_DIM
_ROPE_DIM = _HEAD_DIM // 4
_ROPE_HALF = _ROPE_DIM // 2
_ROPE_THETA = 500000.0
_SGU_GROUPS = 8
_CHUNK = 128
_DEPTH = 1
_ALPHA = (2.0 * _DEPTH) ** 0.25
_LN_EPS = 1e-5
_SEG = 2048

_TM = 1024
_TN = 512
_TN_BF16 = 1024
_ROW_SPLITS = 4
_CAST_COL_BLOCKS = 8
_CAST_PIECES = 8


def _params(sem):
    return pltpu.CompilerParams(dimension_semantics=sem, vmem_limit_bytes=_VMEM_LIMIT_BYTES)


def _dot(a, b):
    return jnp.dot(a, b, preferred_element_type=_F32)


def _layer_norm_rows(x, g, b):
    mu = jnp.mean(x, axis=-1, keepdims=True)
    xc = x - mu
    var = jnp.mean(xc * xc, axis=-1, keepdims=True)
    return xc * lax.rsqrt(var + _LN_EPS) * g + b


def _cast_kernel(x_ref, o_ref):
    o_ref[...] = x_ref[...].astype(o_ref.dtype)


def _cast_bf16(x, tm=512):
    m, n = x.shape
    return pl.pallas_call(
        _cast_kernel,
        grid=(m // tm,),
        in_specs=[pl.BlockSpec((tm, n), lambda i: (i, 0))],
        out_specs=pl.BlockSpec((tm, n), lambda i: (i, 0)),
        out_shape=jax.ShapeDtypeStruct((m, n), _BF16),
        compiler_params=_params(("arbitrary",)),
        name="cast_x",
    )(x)


def _linear_kernel(a_ref, w_ref, *rest, epilogue, row_splits, n_extra, n_cast):
    extras = rest[:n_extra]
    cast_in = rest[n_extra:n_extra + n_cast]
    o_ref = rest[n_extra + n_cast]
    cast_out = rest[n_extra + n_cast + 1:]
    wb = w_ref[...].astype(_BF16)
    slab = a_ref.shape[0] // row_splits
    for s in range(row_splits):
        rows = slice(s * slab, (s + 1) * slab)
        epilogue(_dot(a_ref[rows, :], wb), rows, o_ref, *extras)
    for src, dst in zip(cast_in, cast_out):
        dst[...] = src[...].astype(dst.dtype)


def _row_spec(tm, width):
    return pl.BlockSpec((tm, width), lambda i, j: (i, 0))


def _col_spec(tn, blk0=0):
    return pl.BlockSpec((1, tn), lambda i, j: (0, blk0 + j))


def _tile_spec(tm, tn, blk0=0):
    return pl.BlockSpec((tm, tn), lambda i, j: (i, blk0 + j))


def _linear(name, a, w, col0, n_cols, epilogue, extras=(), out_dtype=_BF16, tm=_TM, tn=_TN,
            transposed_out=False, row_splits=1, convert=()):
    m, k = a.shape
    blk0 = col0 // tn
    n_i, n_j = m // tm, n_cols // tn
    arrays = [x for x, _ in extras]
    specs = [s for _, s in extras]
    if transposed_out:
        out_spec = pl.BlockSpec((tn, tm), lambda i, j: (j, i))
        out_shape = jax.ShapeDtypeStruct((n_cols, m), out_dtype)
    else:
        out_spec = pl.BlockSpec((tm, tn), lambda i, j: (i, j))
        out_shape = jax.ShapeDtypeStruct((m, n_cols), out_dtype)
    band_specs = [pl.BlockSpec((c.shape[0] // (n_i * n_j), c.shape[1]),
                               lambda i, j: (i * n_j + j, 0)) for c in convert]
    outs = pl.pallas_call(
        functools.partial(_linear_kernel, epilogue=epilogue, row_splits=row_splits,
                          n_extra=len(extras), n_cast=len(convert)),
        grid=(n_i, n_j),
        in_specs=[
            pl.BlockSpec((tm, k), lambda i, j: (i, 0)),
            pl.BlockSpec((k, tn), lambda i, j: (0, blk0 + j)),
            *specs,
            *band_specs,
        ],
        out_specs=[out_spec, *band_specs],
        out_shape=[out_shape, *[jax.ShapeDtypeStruct(c.shape, _BF16) for c in convert]],
        compiler_params=_params(("arbitrary", "arbitrary")),
        name=name,
    )(a, w, *arrays, *convert)
    return (outs[0], outs[1:]) if convert else outs[0]


def _transpose_epilogue(acc, rows, o_ref):
    o_ref[:, rows] = acc.T.astype(o_ref.dtype)


def _gelu_epilogue(acc, rows, o_ref):
    o_ref[rows, :] = jax.nn.gelu(acc).astype(o_ref.dtype)


def _sigmoid_epilogue(acc, rows, o_ref):
    o_ref[rows, :] = jax.nn.sigmoid(acc).astype(o_ref.dtype)


def _rope_epilogue(acc, rows, o_ref, cos_ref, sup_ref, sdn_ref, scale_ref):
    cos, sup, sdn = cos_ref[rows, :], sup_ref[rows, :], sdn_ref[rows, :]
    for c in range(acc.shape[1] // _LANES):
        sl = slice(c * _LANES, (c + 1) * _LANES)
        t = acc[:, sl]
        r = (t * cos
             + pltpu.roll(t, _ROPE_HALF, 1) * sup
             + pltpu.roll(t, _LANES - _ROPE_HALF, 1) * sdn)
        o_ref[rows, sl] = (r * scale_ref[:, sl]).astype(o_ref.dtype)


def _residual_epilogue(acc, rows, o_ref, x_ref):
    o_ref[rows, :] = _ALPHA * x_ref[rows, :] + acc


def _relu2_epilogue(acc, rows, o_ref, b_ref):
    z = jnp.maximum(acc + b_ref[...], 0.0)
    o_ref[rows, :] = (z * z).astype(o_ref.dtype)


def _attn_kernel(qi_tab, ki_tab, lq1_ref, lk1_ref, lq2_ref, lk2_ref, subw_ref,
                 q_ref, k_ref, vt_ref, *rest, n_cast, tq, tk, rq, ks, kh, ahead, lambda_init):
    cast_in = rest[:n_cast]
    o_ref = rest[n_cast]
    cast_out = rest[n_cast + 1:2 * n_cast + 1]
    m_ref, l_ref, acc_ref = rest[2 * n_cast + 1:]
    p_idx = pl.program_id(1)
    qi = qi_tab[p_idx]
    ki = ki_tab[p_idx]

    @pl.when(ki == 0)
    def _():
        m_ref[...] = jnp.full(m_ref.shape, -jnp.inf, _F32)
        l_ref[...] = jnp.zeros(l_ref.shape, _F32)
        acc_ref[...] = jnp.zeros(acc_ref.shape, _F32)

    def step(off):

        def span_off(b):
            return None if off is None else off + b * ks

        def keys_used(r, b):
            if off is None:
                return ks
            last_q = (r + 1) * rq - 1
            return kh * sum(1 for h in range(ks // kh) if span_off(b) + h * kh <= last_q)

        def needs_mask(r, b):
            return off is not None and span_off(b) + keys_used(r, b) - 1 > r * rq

        chunks = [(c, r, b) for b in range(tk // ks) for r in range(tq // rq) for c in range(2)
                  if keys_used(r, b) > 0]
        is_last_block = off is not None and off + tk >= tq
        last_chunk_of = {r: i for i, (_, r, _) in enumerate(chunks)}

        def scores(c, r, b):
            sl = slice(c * _HEAD_DIM, (c + 1) * _HEAD_DIM)
            q = q_ref[r * rq:(r + 1) * rq, sl]
            parts = [lax.dot_general(k_ref[b * ks + h * kh:b * ks + (h + 1) * kh, sl], q,
                                     (((1,), (1,)), ((), ())), preferred_element_type=_F32)
                     for h in range(keys_used(r, b) // kh)]
            return jnp.concatenate(parts, axis=0)

        def softmax_pv(c, r, b, st):
            qs = slice(r * rq, (r + 1) * rq)
            nk = keys_used(r, b)
            if needs_mask(r, b):
                delta = (lax.broadcasted_iota(jnp.int32, (nk, rq), 0)
                         - lax.broadcasted_iota(jnp.int32, (nk, rq), 1))
                st = jnp.where(delta <= r * rq - span_off(b), st, -jnp.inf)
            m_prev = m_ref[c, :, qs]
            m_new = jnp.maximum(m_prev, jnp.max(st, axis=0, keepdims=True))
            alpha = jnp.exp2(m_prev - m_new)
            p = jnp.exp2(st - m_new)
            l_ref[c, :, qs] = alpha * l_ref[c, :, qs] + jnp.sum(p, axis=0, keepdims=True)
            acc_ref[c, :, qs] = (alpha * acc_ref[c, :, qs]
                                 + _dot(vt_ref[:, b * ks:b * ks + nk], p.astype(_BF16)))
            m_ref[c, :, qs] = m_new

        pending = [scores(*ch) for ch in chunks[:ahead]]
        def cast_piece(i):
            for src, dst in zip(cast_in, cast_out):
                rows = src.shape[0] // _CAST_PIECES
                dst[i * rows:(i + 1) * rows, :] = src[i * rows:(i + 1) * rows, :].astype(dst.dtype)

        assert len(chunks) >= _CAST_PIECES
        if is_last_block:
            for r in range(tq // rq):
                if r not in last_chunk_of:
                    write_out(r)
        for i, ch in enumerate(chunks):
            if i + ahead < len(chunks):
                pending.append(scores(*chunks[i + ahead]))
            softmax_pv(*ch, pending.pop(0))
            if i < _CAST_PIECES:
                cast_piece(i)
            if is_last_block and last_chunk_of[ch[1]] == i:
                write_out(ch[1])

    def write_out(r):
        qs = slice(r * rq, (r + 1) * rq)
        lam = (jnp.exp(jnp.sum(lq1_ref[...] * lk1_ref[...], axis=1, keepdims=True))
               - jnp.exp(jnp.sum(lq2_ref[...] * lk2_ref[...], axis=1, keepdims=True))
               + lambda_init)
        ot = (acc_ref[0, :, qs] / l_ref[0, :, qs]
              - lam * (acc_ref[1, :, qs] / l_ref[1, :, qs]))
        ot = ot * lax.rsqrt(jnp.mean(ot * ot, axis=0, keepdims=True) + _LN_EPS)
        ot = ot * subw_ref[...] * (1.0 - lambda_init)
        o_ref[qs, :] = ot.T.astype(o_ref.dtype)

    off = ki * tk - qi * tq

    @pl.when(off + tk <= 0)
    def _():
        step(None)

    for diag_off in range(0, tq, tk):
        @pl.when(off == diag_off)
        def _(diag_off=diag_off):
            step(diag_off)


def _causal_pairs(n_q, tq, tk):
    qs, ks = [], []
    for qi in range(n_q):
        for ki in range(((qi + 1) * tq - 1) // tk + 1):
            qs.append(qi)
            ks.append(ki)
    return np.asarray(qs, np.int32), np.asarray(ks, np.int32)


def _attention(qk, vt, lq1, lk1, lq2, lk2, subw_col, lambda_init, cast_weights,
               tq=2048, tk=2048, rq=512, ks=1024, kh=512, ahead=2):
    s_len = qk.shape[0]
    q_blocks, k_blocks = _causal_pairs(s_len // tq, tq, tk)
    n_pairs = len(q_blocks)
    assert n_pairs >= _CAST_COL_BLOCKS
    k_blk0 = _SEG // _V_DIM
    vec = lambda n: pl.BlockSpec((1, n), lambda h, p, qt, kt: (0, 0))

    def cast_specs_of(w, col_lo, col_hi):
        rows, cols = w.shape[0] // _HEADS, (col_hi - col_lo) // _CAST_COL_BLOCKS
        blk0 = col_lo // cols
        band = lambda p: jnp.minimum(p, _CAST_COL_BLOCKS - 1)
        return (pl.BlockSpec((rows, cols), lambda h, p, qt, kt: (h, blk0 + band(p))),
                pl.BlockSpec((rows, cols), lambda h, p, qt, kt: (h, band(p))))

    cast_in_specs, cast_out_specs = zip(*[cast_specs_of(*cw) for cw in cast_weights])
    cast_arrays = [w for w, _, _ in cast_weights]
    kern = functools.partial(_attn_kernel, n_cast=len(cast_weights), tq=tq, tk=tk, rq=rq, ks=ks,
                             kh=kh, ahead=ahead, lambda_init=lambda_init)
    outs = pl.pallas_call(
        kern,
        grid_spec=pltpu.PrefetchScalarGridSpec(
            num_scalar_prefetch=2,
            grid=(_HEADS, n_pairs),
            in_specs=[
                vec(_HEAD_DIM), vec(_HEAD_DIM), vec(_HEAD_DIM), vec(_HEAD_DIM),
                pl.BlockSpec((_V_DIM, 1), lambda h, p, qt, kt: (0, 0)),
                pl.BlockSpec((tq, _V_DIM), lambda h, p, qt, kt: (qt[p], h)),
                pl.BlockSpec((tk, _V_DIM), lambda h, p, qt, kt: (kt[p], k_blk0 + h)),
                pl.BlockSpec((_V_DIM, tk), lambda h, p, qt, kt: (h, kt[p])),
                *cast_in_specs,
            ],
            out_specs=[pl.BlockSpec((tq, _V_DIM), lambda h, p, qt, kt: (qt[p], h)),
                       *cast_out_specs],
            scratch_shapes=[
                pltpu.VMEM((2, 1, tq), _F32),
                pltpu.VMEM((2, 1, tq), _F32),
                pltpu.VMEM((2, _V_DIM, tq), _F32),
            ],
        ),
        out_shape=[jax.ShapeDtypeStruct((s_len, _HEADS * _V_DIM), _BF16),
                   *[jax.ShapeDtypeStruct((w.shape[0], hi - lo), _BF16)
                     for w, lo, hi in cast_weights]],
        compiler_params=_params(("arbitrary", "arbitrary")),
        name="diff_attn",
    )(jnp.asarray(q_blocks), jnp.asarray(k_blocks), lq1, lk1, lq2, lk2, subw_col, qk, qk, vt,
      *cast_arrays)
    return outs[0], outs[1:]


def _sgu_kernel(u_ref, s_ref, g_ref, b_ref, w_ref, bs_ref, o_ref, *, tm):
    gd = _SEG // _SGU_GROUPS
    sn = _layer_norm_rows(s_ref[...].astype(_F32), g_ref[...], b_ref[...]).astype(_BF16)
    row = lax.broadcasted_iota(jnp.int32, (_CHUNK, _CHUNK), 0)
    col = lax.broadcasted_iota(jnp.int32, (_CHUNK, _CHUNK), 1)
    for g in range(_SGU_GROUPS):
        w = jnp.where(row >= col, w_ref[g], 0.0).astype(_BF16)
        bias = bs_ref[:, g:g + 1]
        cs = slice(g * gd, (g + 1) * gd)
        for c in range(tm // _CHUNK):
            rs = slice(c * _CHUNK, (c + 1) * _CHUNK)
            mixed = _dot(w, sn[rs, cs]) + bias
            o_ref[rs, cs] = (u_ref[rs, cs].astype(_F32) * mixed).astype(o_ref.dtype)


def _sgu(us, ln_g, ln_b, w_s, b_s_t, tm=1024):
    s_len = us.shape[0]
    return pl.pallas_call(
        functools.partial(_sgu_kernel, tm=tm),
        grid=(s_len // tm,),
        in_specs=[
            pl.BlockSpec((tm, _SEG), lambda i: (i, 0)),
            pl.BlockSpec((tm, _SEG), lambda i: (i, 1)),
            pl.BlockSpec((1, _SEG), lambda i: (0, 0)),
            pl.BlockSpec((1, _SEG), lambda i: (0, 0)),
            pl.BlockSpec((_SGU_GROUPS, _CHUNK, _CHUNK), lambda i: (0, 0, 0)),
            pl.BlockSpec((_CHUNK, _SGU_GROUPS), lambda i: (0, 0)),
        ],
        out_specs=pl.BlockSpec((tm, _SEG), lambda i: (i, 0)),
        out_shape=jax.ShapeDtypeStruct((s_len, _SEG), _BF16),
        compiler_params=_params(("arbitrary",)),
        name="sgu",
    )(us, us, ln_g, ln_b, w_s, b_s_t)


def _merge_kernel(ya_ref, yb_ref, wa_ref, wb_ref, ga_ref, gb_ref, o_ref):
    wa = wa_ref[...].astype(_BF16)
    wb = wb_ref[...].astype(_BF16)
    slab = ya_ref.shape[0] // _ROW_SPLITS
    for s in range(_ROW_SPLITS):
        rows = slice(s * slab, (s + 1) * slab)
        a = _dot(ya_ref[rows, :], wa)
        b = _dot(yb_ref[rows, :], wb)
        o_ref[rows, :] = (ga_ref[rows, :].astype(_F32) * a
                          + gb_ref[rows, :].astype(_F32) * b).astype(o_ref.dtype)


def _merge(y_a, y_b, w_a, w_b, gates, tm=_TM, tn=_TN_BF16):
    m, k = y_a.shape
    n = w_a.shape[1]
    return pl.pallas_call(
        _merge_kernel,
        grid=(m // tm, n // tn),
        in_specs=[
            pl.BlockSpec((tm, k), lambda i, j: (i, 0)),
            pl.BlockSpec((tm, k), lambda i, j: (i, 0)),
            pl.BlockSpec((k, tn), lambda i, j: (0, j)),
            pl.BlockSpec((k, tn), lambda i, j: (0, j)),
            _tile_spec(tm, tn),
            _tile_spec(tm, tn, n // tn),
        ],
        out_specs=pl.BlockSpec((tm, tn), lambda i, j: (i, j)),
        out_shape=jax.ShapeDtypeStruct((m, n), _BF16),
        compiler_params=_params(("arbitrary", "arbitrary")),
        name="merge",
    )(y_a, y_b, w_a, w_b, gates, gates)


def _ln_bf16_kernel(x_ref, g_ref, b_ref, o_ref):
    o_ref[...] = _layer_norm_rows(x_ref[...], g_ref[...], b_ref[...]).astype(o_ref.dtype)


def _ln_bf16(x, g, b, tm=512):
    m, n = x.shape
    row = pl.BlockSpec((tm, n), lambda i: (i, 0))
    vec = pl.BlockSpec((1, n), lambda i: (0, 0))
    return pl.pallas_call(
        _ln_bf16_kernel,
        grid=(m // tm,),
        in_specs=[row, vec, vec],
        out_specs=row,
        out_shape=jax.ShapeDtypeStruct((m, n), _BF16),
        compiler_params=_params(("arbitrary",)),
        name="ln1",
    )(x, g, b)


_RING = 3


def _final_kernel(pre_hbm, ff_hbm, g1_ref, b1_ref, bo_ref, g2_ref, b2_ref, o_ref,
                  pre_buf, ff_buf, sem, *, tm, n_steps):
    s = pl.program_id(0)

    def copies(step):
        slot = step % _RING
        rows = pl.ds(pl.multiple_of(step * tm, tm), tm)
        return (pltpu.make_async_copy(pre_hbm.at[rows, :], pre_buf.at[slot], sem.at[0, slot]),
                pltpu.make_async_copy(ff_hbm.at[rows, :], ff_buf.at[slot], sem.at[1, slot]))

    def start(step):
        for c in copies(step):
            c.start()

    @pl.when(s == 0)
    def _():
        for k in range(_RING - 1):
            start(k)

    @pl.when(s + (_RING - 1) < n_steps)
    def _():
        start(s + (_RING - 1))

    for c in copies(s):
        c.wait()

    slot = s % _RING
    x = pre_buf[slot]
    xc = x - jnp.mean(x, axis=-1, keepdims=True)
    var = jnp.mean(xc * xc, axis=-1, keepdims=True)
    shift = _ALPHA * b1_ref[...] + bo_ref[...]
    y = xc * (_ALPHA * lax.rsqrt(var + _LN_EPS)) * g1_ref[...] + (ff_buf[slot] + shift)
    o_ref[...] = _layer_norm_rows(y, g2_ref[...], b2_ref[...])


def _final(pre1, ff, g1, b1, b_out, g2, b2, tm=256):
    m, n = pre1.shape
    n_steps = m // tm
    assert n_steps >= _RING
    hbm = pl.BlockSpec(memory_space=pl.ANY)
    vec = pl.BlockSpec((1, n), lambda i: (0, 0))
    return pl.pallas_call(
        functools.partial(_final_kernel, tm=tm, n_steps=n_steps),
        grid=(n_steps,),
        in_specs=[hbm, hbm, vec, vec, vec, vec, vec],
        out_specs=pl.BlockSpec((tm, n), lambda i: (i, 0)),
        out_shape=jax.ShapeDtypeStruct((m, n), _F32),
        scratch_shapes=[pltpu.VMEM((_RING, tm, n), _F32), pltpu.VMEM((_RING, tm, n), _F32),
                        pltpu.SemaphoreType.DMA((2, _RING))],
        compiler_params=_params(("arbitrary",)),
        name="final_ln",
    )(pre1, ff, g1, b1, b_out, g2, b2)


def _mlp_out_kernel(z_ref, w_ref, o_ref):
    kk = pl.program_id(2)
    slab = z_ref.shape[0] // _ROW_SPLITS

    def accumulate(first):
        wb = w_ref[...].astype(_BF16)
        for s in range(_ROW_SPLITS):
            rows = slice(s * slab, (s + 1) * slab)
            part = _dot(z_ref[rows, :], wb)
            o_ref[rows, :] = part if first else o_ref[rows, :] + part

    @pl.when(kk == 0)
    def _():
        accumulate(True)

    @pl.when(kk > 0)
    def _():
        accumulate(False)


def _mlp_out(z, w, tm=2048, tn=1024, tk=2048):
    m, k = z.shape
    n = w.shape[1]
    return pl.pallas_call(
        _mlp_out_kernel,
        grid=(m // tm, n // tn, k // tk),
        in_specs=[
            pl.BlockSpec((tm, tk), lambda i, j, kk: (i, kk)),
            pl.BlockSpec((tk, tn), lambda i, j, kk: (kk, j)),
        ],
        out_specs=pl.BlockSpec((tm, tn), lambda i, j, kk: (i, j)),
        out_shape=jax.ShapeDtypeStruct((m, n), _F32),
        compiler_params=_params(("arbitrary", "arbitrary", "arbitrary")),
        name="mlp_out",
    )(z, w)


def _rope_tables(s_len):
    inv_freq = _ROPE_THETA ** (-np.arange(0, _ROPE_DIM, 2, dtype=np.float64) / _ROPE_DIM)
    ang = np.arange(s_len, dtype=np.float64)[:, None] * inv_freq[None, :]
    cos, sin = np.cos(ang), np.sin(ang)
    ones = np.ones((s_len, _LANES - _ROPE_DIM))
    zeros_h = np.zeros((s_len, _ROPE_HALF))
    zeros_r = np.zeros((s_len, _LANES - _ROPE_DIM))
    cos_t = np.concatenate([cos, cos, ones], axis=1)
    sin_up = np.concatenate([zeros_h, sin, zeros_r], axis=1)
    sin_dn = np.concatenate([-sin, zeros_h, zeros_r], axis=1)
    return tuple(jnp.asarray(t, dtype=_F32) for t in (cos_t, sin_up, sin_dn))


def kernel(x, w_in, lambda_q1, lambda_k1, lambda_q2, lambda_k2, subln_w, sgu_ln_g, sgu_ln_b,
           w_spatial, b_spatial, w_proj_attn, w_proj_sgu, w_out, ln1_g, ln1_b, w_mlp_in, b_mlp_in,
           w_mlp_out, b_mlp_out, ln2_g, ln2_b):
    bsz, s_len, d = x.shape
    assert bsz == 1 and w_in.shape[0] == _DEPTH == 1
    h = x.reshape(s_len, d)
    l = 0
    lambda_init = 0.8 - 0.6 * math.exp(-0.3 * l)
    cos_t, sin_up, sin_dn = _rope_tables(s_len)
    q_scale = _HEAD_DIM ** -0.5 * math.log2(math.e)
    qk_scale = jnp.concatenate([jnp.full((1, _SEG), q_scale, _F32), jnp.ones((1, _SEG), _F32)], axis=1)

    hb = _cast_bf16(h)
    w_in_l = w_in[l]
    qk = _linear("proj_qk", hb, w_in_l, 0, 2 * _SEG, _rope_epilogue,
                 extras=[(cos_t, _row_spec(_TM, _LANES)), (sin_up, _row_spec(_TM, _LANES)),
                         (sin_dn, _row_spec(_TM, _LANES)), (qk_scale, _col_spec(_TN))],
                 row_splits=2 * _ROW_SPLITS)
    vt = _linear("proj_v", hb, w_in_l, 2 * _SEG, _SEG, _transpose_epilogue, transposed_out=True)

    y_a, (w_tail,) = _attention(
        qk, vt, lambda_q1[l][None], lambda_k1[l][None], lambda_q2[l][None], lambda_k2[l][None],
        subln_w[l][:, None], lambda_init, cast_weights=[(w_in_l, 3 * _SEG, w_in_l.shape[1])])
    us, (w_pa, w_ps, w_o) = _linear("proj_us", hb, w_tail, 0, 2 * _SEG, _gelu_epilogue,
                                    tn=_TN_BF16,
                                    convert=[w_proj_attn[l], w_proj_sgu[l], w_out[l]])
    gates, (w_mi,) = _linear("proj_gates", hb, w_tail, 2 * _SEG, 2 * d, _sigmoid_epilogue,
                             tn=_TN_BF16, row_splits=_ROW_SPLITS, convert=[w_mlp_in[l]])
    y_b = _sgu(us, sgu_ln_g[l][None], sgu_ln_b[l][None], w_spatial[l], b_spatial[l].T)
    merged = _merge(y_a, y_b, w_pa, w_ps, gates)
    pre1 = _linear("outproj", merged, w_o, 0, d, _residual_epilogue,
                   extras=[(h, _tile_spec(_TM, _TN_BF16))], out_dtype=_F32, tn=_TN_BF16)
    h1b = _ln_bf16(pre1, ln1_g[l][None], ln1_b[l][None])
    z, (w_mo,) = _linear("mlp_in", h1b, w_mi, 0, w_mi.shape[1], _relu2_epilogue,
                         extras=[(b_mlp_in[l][None], _col_spec(_TN_BF16))], tn=_TN_BF16,
                         convert=[w_mlp_out[l]])
    ff = _mlp_out(z, w_mo)
    out = _final(pre1, ff, ln1_g[l][None], ln1_b[l][None], b_mlp_out[l][None],
                 ln2_g[l][None], ln2_b[l][None])
    return out.reshape(bsz, s_len, d)
```
